```python
import math
import jax, jax.numpy as jnp
from jax import lax
import numpy as np

D_MODEL = 1024
BATCH = 2
SEQ = 8192
DEPTH = 4
DEC_BATCH = 128
DEC_SEQ = 4
PAST_LEN = 2048
PAGE_SIZE = 128

D_MIX = D_MODEL
HG_WIDTH = D_MIX // 2
HG_HEADS = 4
HG_DK = HG_WIDTH // HG_HEADS
HG_DV = HG_WIDTH // HG_HEADS
HG_CHUNK = 64
MB_WIDTH = D_MIX - HG_WIDTH
MB_HEADS = 8
MB_DH = MB_WIDTH // MB_HEADS
MB_BLOCK = 256
MB_TOPK = 3
MB_QBLOCK = 128
D_FF = ((8 * D_MODEL // 3 + 127) // 128) * 128
FFN_SCALE = 0.5
EPS = 1e-6

kernel_name = 'hymba_hgrn2_moba_macaron_step'


def rmsnorm(x, g):
    xf = x.astype(jnp.float32)
    y = xf * lax.rsqrt(jnp.mean(xf * xf, axis=-1, keepdims=True) + EPS)
    return (y * g.astype(jnp.float32)).astype(x.dtype)


def swiglu(x, wi, wo):
    a, b = jnp.split(x @ wi, 2, axis=-1)
    return (jax.nn.silu(a) * b) @ wo


def hgrn2_lower_bounds(gamma):
    p = jax.nn.softmax(gamma.astype(jnp.float32), axis=0)
    return jnp.cumsum(p, axis=0) - p[:1]


def hgrn2_scan(q, k, v, logf, s0):
    B, T, H, K = q.shape
    V = v.shape[-1]
    C = HG_CHUNK if T % HG_CHUNK == 0 else T
    n = T // C

    def to_chunks(a):
        return a.reshape(B, n, C, H, a.shape[-1]).transpose(1, 0, 3, 2, 4)

    causal = jnp.tril(jnp.ones((C, C), dtype=bool))

    def step(S, inp):
        qc, kc, vc, gc = inp
        b = jnp.cumsum(gc, axis=2)
        diff = b[:, :, :, None, :] - b[:, :, None, :, :]
        decay = jnp.exp(jnp.where(causal[:, :, None], diff, -jnp.inf))
        a = jnp.einsum('bhtk,bhsk,bhtsk->bhts', qc, kc, decay)
        o = jnp.einsum('bhts,bhsv->bhtv', a, vc) + jnp.einsum('bhtk,bhkv->bhtv', qc * jnp.exp(b), S)
        b_last = b[:, :, -1:, :]
        S = jnp.exp(b_last[:, :, 0, :])[..., None] * S + jnp.einsum('bhsk,bhsv->bhkv', kc * jnp.exp(b_last - b), vc)
        return S, o

    S, o = lax.scan(step, s0, (to_chunks(q), to_chunks(k), to_chunks(v), to_chunks(logf)))
    return o.transpose(1, 0, 3, 2, 4).reshape(B, T, H, V), S


def moba_attention(q, k, v, q_start, q_block):
    B, Tq, H, Dh = q.shape
    L = k.shape[1]
    nkb = max(-(-L // MB_BLOCK), MB_TOPK)
    pad = nkb * MB_BLOCK + q_block - L
    kp = jnp.pad(k, ((0, 0), (0, pad), (0, 0), (0, 0)))
    vp = jnp.pad(v, ((0, 0), (0, pad), (0, 0), (0, 0)))
    kb = kp[:, :nkb * MB_BLOCK].reshape(B, nkb, MB_BLOCK, H, Dh)
    vb = vp[:, :nkb * MB_BLOCK].reshape(B, nkb, MB_BLOCK, H, Dh)
    kmean = jnp.mean(kb.astype(jnp.float32), axis=2)
    win = MB_BLOCK + q_block - 1
    nqc = Tq // q_block
    q_chunks = q.reshape(B, nqc, q_block, H, Dh).transpose(1, 0, 2, 3, 4)
    starts = q_start + q_block * jnp.arange(nqc, dtype=jnp.int32)
    b_ix = jnp.arange(B)[:, None, None, None]
    h_ix = jnp.arange(H)[None, :, None, None]
    blk = jnp.arange(nkb)
    scale = Dh ** -0.5
    n_sel = MB_TOPK * MB_BLOCK

    def attend_chunk(args):
        qb, p0 = args
        qpos = p0 + jnp.arange(q_block, dtype=jnp.int32)
        own = qpos // MB_BLOCK
        gate = jnp.einsum('bqhd,bnhd->bhqn', qb.astype(jnp.float32), kmean)
        gate = jnp.where(blk[None, :] < own[:, None], gate, -jnp.inf)
        _, sel = lax.top_k(gate, MB_TOPK)
        sel_ok = sel < own[:, None]
        ks = kb[b_ix, sel, :, h_ix]
        vs = vb[b_ix, sel, :, h_ix]
        s_sel = jnp.einsum('bqhd,bhqnkd->bhqnk', qb, ks).astype(jnp.float32) * scale
        s_sel = jnp.where(sel_ok[..., None], s_sel, -jnp.inf).reshape(B, H, q_block, n_sel)
        w0 = (p0 // MB_BLOCK) * MB_BLOCK
        kw = lax.dynamic_slice_in_dim(kp, w0, win, axis=1)
        vw = lax.dynamic_slice_in_dim(vp, w0, win, axis=1)
        wpos = w0 + jnp.arange(win, dtype=jnp.int32)
        own_ok = (wpos[None, :] // MB_BLOCK == own[:, None]) & (wpos[None, :] <= qpos[:, None])
        s_own = jnp.einsum('bqhd,bkhd->bhqk', qb, kw).astype(jnp.float32) * scale
        s_own = jnp.where(own_ok, s_own, -jnp.inf)
        p = jax.nn.softmax(jnp.concatenate([s_sel, s_own], axis=-1), axis=-1).astype(v.dtype)
        p_sel = p[..., :n_sel].reshape(B, H, q_block, MB_TOPK, MB_BLOCK)
        o = jnp.einsum('bhqnk,bhqnkd->bqhd', p_sel, vs) + jnp.einsum('bhqk,bkhd->bqhd', p[..., n_sel:], vw)
        return o

    o = lax.map(attend_chunk, (q_chunks, starts))
    return o.transpose(1, 0, 2, 3, 4).reshape(B, Tq, H * Dh)


def hybrid_layer(x, lb, w_in, w_out, hg_gnorm, n_ffn1, ffn1_wi, ffn1_wo, n_mix, n_ffn2, ffn2_wi, ffn2_wo,
                 hg_state, past_k, past_v):
    B, T, _ = x.shape
    x = x + FFN_SCALE * swiglu(rmsnorm(x, n_ffn1), ffn1_wi, ffn1_wo)
    h = rmsnorm(x, n_mix)
    proj = h @ w_in
    offs = [HG_WIDTH, 2 * HG_WIDTH, 3 * HG_WIDTH, 4 * HG_WIDTH,
            4 * HG_WIDTH + MB_WIDTH, 4 * HG_WIDTH + 2 * MB_WIDTH]
    hq, hf, hi, hg, mq, mk, mv = jnp.split(proj, offs, axis=-1)
    hshape = (B, T, HG_HEADS, HG_DK)
    z = hf.astype(jnp.float32).reshape(hshape)
    lbh = lb.reshape(HG_HEADS, HG_DK)
    f = lbh + (1.0 - lbh) * jax.nn.sigmoid(z)
    k_hg = (1.0 - lbh) * jax.nn.sigmoid(-z)
    q_hg = jax.nn.silu(hq.astype(jnp.float32)).reshape(hshape)
    v_hg = hi.astype(jnp.float32).reshape(B, T, HG_HEADS, HG_DV)
    o_hg, s_new = hgrn2_scan(q_hg, k_hg, v_hg, jnp.log(f), hg_state.astype(jnp.float32))
    gate = jax.nn.silu(hg.astype(jnp.float32)).reshape(B, T, HG_HEADS, HG_DV)
    o_hg = (rmsnorm(o_hg, hg_gnorm) * gate).reshape(B, T, HG_WIDTH).astype(x.dtype)
    mshape = (B, T, MB_HEADS, MB_DH)
    q_mb, k_mb, v_mb = mq.reshape(mshape), mk.reshape(mshape), mv.reshape(mshape)
    if past_k is None:
        o_mb = moba_attention(q_mb, k_mb, v_mb, 0, MB_QBLOCK)
    else:
        k_all = jnp.concatenate([past_k.astype(k_mb.dtype), k_mb], axis=1)
        v_all = jnp.concatenate([past_v.astype(v_mb.dtype), v_mb], axis=1)
        o_mb = moba_attention(q_mb, k_all, v_all, past_k.shape[1], 1)
    y = jnp.concatenate([o_hg, o_mb.astype(x.dtype)], axis=-1) @ w_out
    x = x + y
    x = x + FFN_SCALE * swiglu(rmsnorm(x, n_ffn2), ffn2_wi, ffn2_wo)
    return x, k_mb, v_mb, s_new


def setup_inputs(seed: int = 0) -> dict:
    key = jax.random.key(seed)
    ks = jax.random.split(key, 24)
    n_pages = PAST_LEN // PAGE_SIZE
    n_pool = (5 * DEC_BATCH * n_pages) // 4
    d_in = 4 * HG_WIDTH + 3 * MB_WIDTH

    def nrm(k, shape, scale):
        return jax.random.normal(k, shape, jnp.float32) * scale

    def gain(k, shape):
        return 1.0 + nrm(k, shape, 0.01)

    page_table = jax.random.permutation(ks[5], n_pool)[:DEC_BATCH * n_pages]
    page_table = page_table.reshape(DEC_BATCH, n_pages).astype(jnp.int32)
    return {
        'x_prompt': nrm(ks[0], (BATCH, SEQ, D_MODEL), 1.0),
        'x_sample': nrm(ks[1], (DEC_BATCH, DEC_SEQ, D_MODEL), 1.0),
        'cache_k': nrm(ks[2], (DEPTH, n_pool, PAGE_SIZE, MB_HEADS, MB_DH), 1.0),
        'cache_v': nrm(ks[3], (DEPTH, n_pool, PAGE_SIZE, MB_HEADS, MB_DH), 1.0),
        'state_hgrn': nrm(ks[4], (DEPTH, DEC_BATCH, HG_HEADS, HG_DK, HG_DV), 0.5),
        'page_table': page_table,
        'hg_lower_bounds': nrm(ks[6], (DEPTH, HG_WIDTH), 0.1),
        'w_in': nrm(ks[7], (DEPTH, D_MODEL, d_in), D_MODEL ** -0.5),
        'w_out': nrm(ks[8], (DEPTH, D_MIX, D_MODEL), D_MIX ** -0.5),
        'hg_gnorm': gain(ks[9], (DEPTH, HG_DV)),
        'norm_ffn1': gain(ks[10], (DEPTH, D_MODEL)),
        'ffn1_wi': nrm(ks[11], (DEPTH, D_MODEL, 2 * D_FF), D_MODEL ** -0.5),
        'ffn1_wo': nrm(ks[12], (DEPTH, D_FF, D_MODEL), D_FF ** -0.5),
        'norm_mix': gain(ks[13], (DEPTH, D_MODEL)),
        'norm_ffn2': gain(ks[14], (DEPTH, D_MODEL)),
        'ffn2_wi': nrm(ks[15], (DEPTH, D_MODEL, 2 * D_FF), D_MODEL ** -0.5),
        'ffn2_wo': nrm(ks[16], (DEPTH, D_FF, D_MODEL), D_FF ** -0.5),
        'final_norm': gain(ks[17], (D_MODEL,)),
    }


def reference(x_prompt, x_sample, cache_k, cache_v, state_hgrn, page_table, hg_lower_bounds, w_in, w_out,
              hg_gnorm, norm_ffn1, ffn1_wi, ffn1_wo, norm_mix, norm_ffn2, ffn2_wi, ffn2_wo, final_norm):
    lower = hgrn2_lower_bounds(hg_lower_bounds)
    n_pages = PAST_LEN // PAGE_SIZE
    yp, ys = x_prompt, x_sample
    kp_rows, vp_rows, sp_rows, ks_rows, vs_rows, ss_rows = [], [], [], [], [], []
    for l in range(DEPTH):
        shared = (lower[l], w_in[l], w_out[l], hg_gnorm[l], norm_ffn1[l], ffn1_wi[l], ffn1_wo[l],
                  norm_mix[l], norm_ffn2[l], ffn2_wi[l], ffn2_wo[l])
        s0 = jnp.zeros((BATCH, HG_HEADS, HG_DK, HG_DV), jnp.float32)
        yp, k_new, v_new, s_new = hybrid_layer(yp, *shared, s0, None, None)
        kp_rows.append(k_new.astype(cache_k.dtype))
        vp_rows.append(v_new.astype(cache_v.dtype))
        sp_rows.append(s_new.astype(state_hgrn.dtype))
        past_k = cache_k[l, page_table].reshape(DEC_BATCH, n_pages * PAGE_SIZE, MB_HEADS, MB_DH)
        past_v = cache_v[l, page_table].reshape(DEC_BATCH, n_pages * PAGE_SIZE, MB_HEADS, MB_DH)
        ys, k_new, v_new, s_new = hybrid_layer(ys, *shared, state_hgrn[l], past_k, past_v)
        ks_rows.append(k_new.astype(cache_k.dtype))
        vs_rows.append(v_new.astype(cache_v.dtype))
        ss_rows.append(s_new.astype(state_hgrn.dtype))
    y_prompt = rmsnorm(yp, final_norm)
    y_sample = rmsnorm(ys, final_norm)
    return (y_prompt, y_sample, jnp.stack(kp_rows), jnp.stack(vp_rows), jnp.stack(sp_rows),
            jnp.stack(ks_rows), jnp.stack(vs_rows), jnp.stack(ss_rows))
```

```python
import functools

import jax
import jax.numpy as jnp
from jax import lax
from jax.experimental import pallas as pl
from jax.experimental.pallas import tpu as pltpu

F32 = jnp.float32
BF16 = jnp.bfloat16

D_MODEL = 1024
HG_HEADS = 4
HG_D = 128
HG_WIDTH = HG_HEADS * HG_D
MB_HEADS = 8
MB_DH = 64
MB_WIDTH = MB_HEADS * MB_DH
MB_BLOCK = 256
MB_TOPK = 3
D_FF = 2816
D_IN = 4 * HG_WIDTH + 3 * MB_WIDTH
PAGE_SIZE = 128
EPS = 1e-6
FFN_SCALE = 0.5

LANES = 128
FF_CHUNK = 1408
DENSE_ROWS = 256
HG_CHUNK = 64
HG_ROWS = 512
HG_DEC_SEQS = 8
VMEM_LIMIT = 56 * 1024 * 1024

NEG_INF = float("-inf")


def _resident(shape):
    return pl.BlockSpec(shape, lambda *_: (0,) * len(shape), pipeline_mode=pl.Buffered(1))


def _params(semantics):
    return pltpu.CompilerParams(dimension_semantics=semantics, vmem_limit_bytes=VMEM_LIMIT)


def _rmsnorm(x, g):
    return x * lax.rsqrt(jnp.mean(x * x, axis=-1, keepdims=True) + EPS) * g


def _silu(x):
    return x * jax.nn.sigmoid(x)


def _dot_nt(a, b, **kw):
    return lax.dot_general(a, b, (((1,), (1,)), ((), ())), preferred_element_type=F32, **kw)


def _dot_tn(a, b, **kw):
    return lax.dot_general(a, b, (((0,), (0,)), ((), ())), preferred_element_type=F32, **kw)


def _swiglu_half_step(x, n_ref, wi_ref, wo_ref):
    h = _rmsnorm(x, n_ref[...]).astype(BF16)
    acc = jnp.zeros_like(x)
    for c in range(D_FF // FF_CHUNK):
        lo = c * FF_CHUNK
        a = jnp.dot(h, wi_ref[:, lo:lo + FF_CHUNK], preferred_element_type=F32)
        b = jnp.dot(h, wi_ref[:, D_FF + lo:D_FF + lo + FF_CHUNK], preferred_element_type=F32)
        g = (_silu(a) * b).astype(BF16)
        acc = acc + jnp.dot(g, wo_ref[lo:lo + FF_CHUNK, :], preferred_element_type=F32)
    return x + FFN_SCALE * acc


def _ffn_inproj_kernel(x_ref, n1_ref, wi_ref, wo_ref, nm_ref, win_ref, x1_ref, proj_ref):
    x1 = _swiglu_half_step(x_ref[...], n1_ref, wi_ref, wo_ref)
    x1_ref[...] = x1
    h = _rmsnorm(x1, nm_ref[...]).astype(BF16)
    proj_ref[...] = jnp.dot(h, win_ref[...], preferred_element_type=F32)


def _ffn_inproj(x, n1, wi, wo, nm, w_in):
    n = x.shape[0]
    rows = pl.BlockSpec((DENSE_ROWS, D_MODEL), lambda i: (i, 0))
    return pl.pallas_call(
        _ffn_inproj_kernel,
        grid=(n // DENSE_ROWS,),
        in_specs=[rows, _resident((1, D_MODEL)), _resident((D_MODEL, 2 * D_FF)), _resident((D_FF, D_MODEL)),
                  _resident((1, D_MODEL)), _resident((D_MODEL, D_IN))],
        out_specs=[rows, pl.BlockSpec((DENSE_ROWS, D_IN), lambda i: (i, 0))],
        out_shape=[jax.ShapeDtypeStruct((n, D_MODEL), F32), jax.ShapeDtypeStruct((n, D_IN), F32)],
        compiler_params=_params(("parallel",)),
        name="ffn_inproj",
    )(x, n1, wi, wo, nm, w_in)


def _outproj_ffn_kernel(x_ref, ohg_ref, omb_ref, wout_ref, n2_ref, wi_ref, wo_ref, fn_ref, y_ref, *, final):
    x = x_ref[...]
    x = x + jnp.dot(ohg_ref[...], wout_ref[:HG_WIDTH, :], preferred_element_type=F32)
    x = x + jnp.dot(omb_ref[...], wout_ref[HG_WIDTH:, :], preferred_element_type=F32)
    x = _swiglu_half_step(x, n2_ref, wi_ref, wo_ref)
    if final:
        x = _rmsnorm(x, fn_ref[...])
    y_ref[...] = x


def _outproj_ffn(x, o_hg, o_mb, w_out, n2, wi, wo, fn, final):
    n = x.shape[0]
    rows = pl.BlockSpec((DENSE_ROWS, D_MODEL), lambda i: (i, 0))
    half = pl.BlockSpec((DENSE_ROWS, HG_WIDTH), lambda i: (i, 0))
    return pl.pallas_call(
        functools.partial(_outproj_ffn_kernel, final=final),
        grid=(n // DENSE_ROWS,),
        in_specs=[rows, half, half, _resident((D_MODEL, D_MODEL)), _resident((1, D_MODEL)),
                  _resident((D_MODEL, 2 * D_FF)), _resident((D_FF, D_MODEL)), _resident((1, D_MODEL))],
        out_specs=rows,
        out_shape=jax.ShapeDtypeStruct((n, D_MODEL), F32),
        compiler_params=_params(("parallel",)),
        name="outproj_ffn",
    )(x, o_hg, o_mb, w_out, n2, wi, wo, fn)


def _segment_products(f, top):
    rows = f.shape[0]
    t = lax.broadcasted_iota(jnp.int32, f.shape, 0)
    gq, gk, tot = f, jnp.ones_like(f), f
    out = {1: (gq, gk)}
    m = 1
    while m < top:
        second = (t & m) != 0
        up = pltpu.roll(tot, m, 0)
        dn = pltpu.roll(tot, rows - m, 0)
        gq = jnp.where(second, gq * up, gq)
        gk = jnp.where(second, gk, gk * dn)
        tot = tot * jnp.where(second, up, dn)
        m *= 2
        out[m] = (gq, gk)
    return out


def _intra_scores(q, k, prods, top):
    rows = q.shape[0]
    t = lax.broadcasted_iota(jnp.int32, (rows, rows), 0)
    s = lax.broadcasted_iota(jnp.int32, (rows, rows), 1)
    ts = t ^ s
    a = jnp.where(t == s, _dot_nt(q.astype(BF16), k.astype(BF16)), 0.0)
    m = 1
    while m < top:
        gq, gk = prods[m]
        p = _dot_nt((q * gq).astype(BF16), (k * gk).astype(BF16))
        a = jnp.where((ts >= m) & (ts < 2 * m) & (t > s), p, a)
        m *= 2
    return a


def _hgrn_inputs(hq, z, lb):
    sig = jax.nn.sigmoid(z)
    f = lb + (1.0 - lb) * sig
    k = (1.0 - lb) * jax.nn.sigmoid(-z)
    return _silu(hq), k, f


def _hgrn_finish(o, hg, gn):
    return _rmsnorm(o, gn) * _silu(hg)


def _hgrn_prompt_kernel(hq_ref, hf_ref, hi_ref, hg_ref, lb_ref, gn_ref, o_ref, s_ref, st_ref):
    step = pl.program_id(1)

    @pl.when(step == 0)
    def _():
        st_ref[...] = jnp.zeros_like(st_ref)

    def chunk_body(j, carry):
        r0 = pl.multiple_of(j * HG_CHUNK, HG_CHUNK)
        rs = pl.ds(r0, HG_CHUNK)
        for h in range(HG_HEADS):
            cs = slice(h * HG_D, (h + 1) * HG_D)
            q, k, f = _hgrn_inputs(hq_ref[rs, cs], hf_ref[rs, cs], lb_ref[:, cs])
            v = hi_ref[rs, cs].astype(BF16)
            prods = _segment_products(f, HG_CHUNK)
            a = _intra_scores(q, k, prods, HG_CHUNK)
            gq, gk = prods[HG_CHUNK]
            st = st_ref[h]
            o = jnp.dot(a.astype(BF16), v, preferred_element_type=F32)
            o = o + _dot_nt((q * gq).astype(BF16), st.astype(BF16))
            st_ref[h] = st * gq[HG_CHUNK - 1:HG_CHUNK, :] + _dot_tn(v, (k * gk).astype(BF16))
            o_ref[rs, cs] = _hgrn_finish(o, hg_ref[rs, cs], gn_ref[...]).astype(o_ref.dtype)
        return carry

    lax.fori_loop(0, HG_ROWS // HG_CHUNK, chunk_body, 0)

    @pl.when(step == pl.num_programs(1) - 1)
    def _():
        for h in range(HG_HEADS):
            s_ref[0, h] = st_ref[h].T


def _hgrn_prompt(proj, lb, gn, batch, seq):
    steps = seq // HG_ROWS

    def col(c):
        return pl.BlockSpec((HG_ROWS, HG_WIDTH), lambda b, i: (b * steps + i, c))

    return pl.pallas_call(
        _hgrn_prompt_kernel,
        grid=(batch, steps),
        in_specs=[col(0), col(1), col(2), col(3), _resident((1, HG_WIDTH)), _resident((1, HG_D))],
        out_specs=[pl.BlockSpec((HG_ROWS, HG_WIDTH), lambda b, i: (b * steps + i, 0)),
                   pl.BlockSpec((1, HG_HEADS, HG_D, HG_D), lambda b, i: (b, 0, 0, 0))],
        out_shape=[jax.ShapeDtypeStruct((batch * seq, HG_WIDTH), BF16),
                   jax.ShapeDtypeStruct((batch, HG_HEADS, HG_D, HG_D), F32)],
        scratch_shapes=[pltpu.VMEM((HG_HEADS, HG_D, HG_D), F32)],
        compiler_params=_params(("parallel", "arbitrary")),
        name="hgrn_prompt",
    )(proj, proj, proj, proj, lb, gn)


def _hgrn_decode_kernel(hq_ref, hf_ref, hi_ref, hg_ref, lb_ref, gn_ref, s0_ref, o_ref, s_ref, *, seq):
    rows = HG_DEC_SEQS * seq
    row = lax.broadcasted_iota(jnp.int32, (rows, HG_D), 0)
    for h in range(HG_HEADS):
        cs = slice(h * HG_D, (h + 1) * HG_D)
        q, k, f = _hgrn_inputs(hq_ref[:, cs], hf_ref[:, cs], lb_ref[:, cs])
        v = hi_ref[:, cs].astype(BF16)
        prods = _segment_products(f, seq)
        a = _intra_scores(q, k, prods, seq)
        gq, gk = prods[seq]
        qg, kg = q * gq, k * gk
        o = jnp.dot(a.astype(BF16), v, preferred_element_type=F32)
        for i in range(HG_DEC_SEQS):
            mine = (row >= i * seq) & (row < (i + 1) * seq)
            s0 = s0_ref[0, i, h]
            o = o + jnp.dot(jnp.where(mine, qg, 0.0).astype(BF16), s0.astype(BF16), preferred_element_type=F32)
            last = (i + 1) * seq - 1
            decay = jnp.broadcast_to(gq[last:last + 1, :], (HG_D, HG_D)).T
            s_ref[i, h] = s0 * decay + _dot_tn(jnp.where(mine, kg, 0.0).astype(BF16), v)
        o_ref[:, cs] = _hgrn_finish(o, hg_ref[:, cs], gn_ref[...]).astype(o_ref.dtype)


def _hgrn_decode(proj, row0, lb, gn, state, layer, batch, seq):
    rows = HG_DEC_SEQS * seq
    blk0 = row0 // rows

    def col(c):
        return pl.BlockSpec((rows, HG_WIDTH), lambda g: (blk0 + g, c))

    return pl.pallas_call(
        functools.partial(_hgrn_decode_kernel, seq=seq),
        grid=(batch // HG_DEC_SEQS,),
        in_specs=[col(0), col(1), col(2), col(3), _resident((1, HG_WIDTH)), _resident((1, HG_D)),
                  pl.BlockSpec((1, HG_DEC_SEQS, HG_HEADS, HG_D, HG_D), lambda g: (layer, g, 0, 0, 0))],
        out_specs=[pl.BlockSpec((rows, HG_WIDTH), lambda g: (g, 0)),
                   pl.BlockSpec((HG_DEC_SEQS, HG_HEADS, HG_D, HG_D), lambda g: (g, 0, 0, 0))],
        out_shape=[jax.ShapeDtypeStruct((batch * seq, HG_WIDTH), BF16),
                   jax.ShapeDtypeStruct((batch, HG_HEADS, HG_D, HG_D), F32)],
        compiler_params=_params(("parallel",)),
        name="hgrn_decode",
    )(proj, proj, proj, proj, lb, gn, state)


def _top_blocks(gate, n_valid):
    lane = lax.broadcasted_iota(jnp.int32, gate.shape, 1)
    lanef = lane.astype(F32)
    g = jnp.where(lane < n_valid, gate, NEG_INF)
    picks = []
    for _ in range(MB_TOPK):
        mx = jnp.max(g, axis=-1, keepdims=True)
        idx = jnp.min(jnp.where(g == mx, lanef, float(LANES)), axis=-1, keepdims=True)
        picks.append(jnp.where(mx > NEG_INF, idx, -1.0))
        g = jnp.where(lanef == idx, NEG_INF, g)
    return picks


def _block_bias(picks, j):
    jf = jnp.asarray(j).astype(F32)
    hit = (picks[0] == jf) | (picks[1] == jf) | (picks[2] == jf)
    return jnp.where(hit, 0.0, NEG_INF)


def _moba_prompt_kernel(q_ref, k_ref, v_ref, o_ref, kbf_ref, vbf_ref, kmean_ref, *, n_blocks):
    i = pl.program_id(2)

    @pl.when(i == 0)
    def _():
        kmean_ref[...] = jnp.zeros_like(kmean_ref)

        def prep(j, carry):
            rs = pl.ds(pl.multiple_of(j * MB_BLOCK, MB_BLOCK), MB_BLOCK)
            kb = k_ref[rs, :]
            kbf_ref[rs, :] = kb.astype(BF16)
            vbf_ref[rs, :] = v_ref[rs, :].astype(BF16)
            kmean_ref[pl.ds(j, 1), :] = jnp.mean(kb, axis=0, keepdims=True)
            return carry

        lax.fori_loop(0, n_blocks, prep, 0)

    lane = lax.broadcasted_iota(jnp.int32, (MB_BLOCK, LANES), 1)
    first = lane < MB_DH
    q = q_ref[...] * (MB_DH ** -0.5)
    qs = (jnp.where(first, q, 0.0), jnp.where(first, 0.0, q))
    picks = [_top_blocks(_dot_nt(qh, kmean_ref[...], precision=lax.Precision.HIGHEST), i) for qh in qs]
    qs = [qh.astype(BF16) for qh in qs]

    own = pl.ds(pl.multiple_of(i * MB_BLOCK, MB_BLOCK), MB_BLOCK)
    k_own, v_own = kbf_ref[own, :], vbf_ref[own, :]
    r = lax.broadcasted_iota(jnp.int32, (MB_BLOCK, MB_BLOCK), 0)
    c = lax.broadcasted_iota(jnp.int32, (MB_BLOCK, MB_BLOCK), 1)
    init = []
    for qh in qs:
        s = jnp.where(c <= r, _dot_nt(qh, k_own), NEG_INF)
        m = jnp.max(s, axis=-1, keepdims=True)
        p = jnp.exp(s - m)
        init += [m, jnp.sum(p, axis=-1, keepdims=True), jnp.dot(p.astype(BF16), v_own, preferred_element_type=F32)]

    def past_block(j, carry):
        rs = pl.ds(pl.multiple_of(j * MB_BLOCK, MB_BLOCK), MB_BLOCK)
        kj, vj = kbf_ref[rs, :], vbf_ref[rs, :]
        out = []
        for h, qh in enumerate(qs):
            m, l, acc = carry[3 * h:3 * h + 3]
            s = _dot_nt(qh, kj) + _block_bias(picks[h], j)
            m_new = jnp.maximum(m, jnp.max(s, axis=-1, keepdims=True))
            alpha = jnp.exp(m - m_new)
            p = jnp.exp(s - m_new)
            out += [m_new, alpha * l + jnp.sum(p, axis=-1, keepdims=True),
                    alpha * acc + jnp.dot(p.astype(BF16), vj, preferred_element_type=F32)]
        return tuple(out)

    _, l0, acc0, _, l1, acc1 = lax.fori_loop(0, i, past_block, tuple(init))
    o_ref[...] = jnp.where(first, acc0 / l0, acc1 / l1).astype(o_ref.dtype)


def _moba_prompt(proj, batch, seq):
    n_blocks = seq // MB_BLOCK
    pairs = MB_WIDTH // LANES
    q0 = 4 * HG_WIDTH // LANES
    return pl.pallas_call(
        functools.partial(_moba_prompt_kernel, n_blocks=n_blocks),
        grid=(batch, pairs, n_blocks),
        in_specs=[pl.BlockSpec((MB_BLOCK, LANES), lambda b, p, i: (b * n_blocks + i, q0 + p)),
                  pl.BlockSpec((seq, LANES), lambda b, p, i: (b, q0 + pairs + p)),
                  pl.BlockSpec((seq, LANES), lambda b, p, i: (b, q0 + 2 * pairs + p))],
        out_specs=pl.BlockSpec((MB_BLOCK, LANES), lambda b, p, i: (b * n_blocks + i, p)),
        out_shape=jax.ShapeDtypeStruct((batch * seq, MB_WIDTH), BF16),
        scratch_shapes=[pltpu.VMEM((seq, LANES), BF16), pltpu.VMEM((seq, LANES), BF16),
                        pltpu.VMEM((LANES, LANES), F32)],
        compiler_params=_params(("parallel", "parallel", "arbitrary")),
        name="moba_prompt",
    )(proj, proj, proj)


def _moba_decode_kernel(pt_ref, q_ref, k_ref, v_ref, *refs, n_pages, seq):
    del pt_ref
    k_pages, v_pages, o_ref = refs[:n_pages], refs[n_pages:2 * n_pages], refs[2 * n_pages]
    pages_per_block = MB_BLOCK // PAGE_SIZE
    n_blocks = n_pages // pages_per_block
    rows = seq * MB_HEADS
    sub = lax.broadcasted_iota(jnp.int32, (MB_HEADS, MB_WIDTH), 0)
    head_of_lane = lax.broadcasted_iota(jnp.int32, (MB_HEADS, MB_WIDTH), 1) // MB_DH
    head_mask = sub == head_of_lane
    big = lax.broadcasted_iota(jnp.int32, (LANES, MB_WIDTH), 0)

    def rows_to_tile(x):
        out = jnp.zeros((LANES, MB_WIDTH), F32)
        for t in range(x.shape[0]):
            out = jnp.where(big == t, jnp.broadcast_to(x[t:t + 1, :], out.shape), out)
        return out

    q = q_ref[0] * (MB_DH ** -0.5)
    qbd = jnp.concatenate(
        [jnp.where(head_mask, jnp.broadcast_to(q[t:t + 1, :], head_mask.shape), 0.0) for t in range(seq)], axis=0)
    qbd16 = qbd.astype(BF16)

    kmeans = []
    for j in range(n_blocks):
        tot = sum(jnp.sum(k_pages[j * pages_per_block + u][0], axis=0, keepdims=True)
                  for u in range(pages_per_block))
        kmeans.append(tot * (1.0 / MB_BLOCK))
    gate = _dot_nt(qbd, rows_to_tile(jnp.concatenate(kmeans, axis=0)), precision=lax.Precision.HIGHEST)
    picks = _top_blocks(gate, n_blocks)

    scores = [_dot_nt(qbd16, k_pages[p][0].astype(BF16)) + _block_bias(picks, p // pages_per_block)
              for p in range(n_pages)]
    r = lax.broadcasted_iota(jnp.int32, (rows, LANES), 0) // MB_HEADS
    c = lax.broadcasted_iota(jnp.int32, (rows, LANES), 1)
    scores.append(jnp.where(c <= r, _dot_nt(qbd16, rows_to_tile(k_ref[0]).astype(BF16)), NEG_INF))
    values = [v_pages[p][0].astype(BF16) for p in range(n_pages)] + [rows_to_tile(v_ref[0]).astype(BF16)]

    m = functools.reduce(jnp.maximum, scores)
    m = jnp.max(m, axis=-1, keepdims=True)
    l = jnp.zeros((rows, 1), F32)
    acc = jnp.zeros((rows, MB_WIDTH), F32)
    for s, v in zip(scores, values):
        p = jnp.exp(s - m)
        l = l + jnp.sum(p, axis=-1, keepdims=True)
        acc = acc + jnp.dot(p.astype(BF16), v, preferred_element_type=F32)
    acc = acc / l
    for t in range(seq):
        own = jnp.where(head_mask, acc[t * MB_HEADS:(t + 1) * MB_HEADS, :], 0.0)
        o_ref[0, t:t + 1, :] = jnp.sum(own, axis=0, keepdims=True)


def _moba_decode(proj_dec, cache_k, cache_v, page_table, layer):
    batch, seq, _ = proj_dec.shape
    n_pages = page_table.shape[1]
    n_pool = cache_k.shape[1]
    ck = cache_k.reshape(cache_k.shape[0] * n_pool, PAGE_SIZE, MB_WIDTH)
    cv = cache_v.reshape(cache_v.shape[0] * n_pool, PAGE_SIZE, MB_WIDTH)
    q0 = 4 * HG_WIDTH // MB_WIDTH

    def new(c):
        return pl.BlockSpec((1, seq, MB_WIDTH), lambda b, pt: (b, 0, q0 + c))

    def page(j):
        return pl.BlockSpec((1, PAGE_SIZE, MB_WIDTH), lambda b, pt: (layer * n_pool + pt[b * n_pages + j], 0, 0))

    pages = [page(j) for j in range(n_pages)]
    grid_spec = pltpu.PrefetchScalarGridSpec(
        num_scalar_prefetch=1,
        grid=(batch,),
        in_specs=[new(0), new(1), new(2)] + pages + pages,
        out_specs=pl.BlockSpec((1, seq, MB_WIDTH), lambda b, pt: (b, 0, 0)),
    )
    return pl.pallas_call(
        functools.partial(_moba_decode_kernel, n_pages=n_pages, seq=seq),
        grid_spec=grid_spec,
        out_shape=jax.ShapeDtypeStruct((batch, seq, MB_WIDTH), F32),
        compiler_params=_params(("parallel",)),
        name="moba_decode",
    )(page_table.reshape(-1), proj_dec, proj_dec, proj_dec, *([ck] * n_pages), *([cv] * n_pages))


def _lower_bounds(gamma):
    p = jax.nn.softmax(gamma.astype(F32), axis=0)
    return jnp.cumsum(p, axis=0) - p[:1]


def kernel(x_prompt, x_sample, cache_k, cache_v, state_hgrn, page_table, hg_lower_bounds, w_in, w_out, hg_gnorm,
           norm_ffn1, ffn1_wi, ffn1_wo, norm_mix, norm_ffn2, ffn2_wi, ffn2_wo, final_norm):
    batch, seq, _ = x_prompt.shape
    dec_batch, dec_seq, _ = x_sample.shape
    depth = w_in.shape[0]
    n_prompt = batch * seq
    lower = _lower_bounds(hg_lower_bounds)
    x = jnp.concatenate([x_prompt.reshape(n_prompt, D_MODEL), x_sample.reshape(dec_batch * dec_seq, D_MODEL)], axis=0)
    fn = final_norm.reshape(1, D_MODEL)
    kp, vp, sp, ks, vs, ss = [], [], [], [], [], []
    k0, v0 = 4 * HG_WIDTH + MB_WIDTH, 4 * HG_WIDTH + 2 * MB_WIDTH
    for l in range(depth):
        lb = lower[l].reshape(1, HG_WIDTH)
        gn = hg_gnorm[l].reshape(1, HG_D)
        x, proj = _ffn_inproj(x, norm_ffn1[l].reshape(1, D_MODEL), ffn1_wi[l].astype(BF16), ffn1_wo[l].astype(BF16),
                              norm_mix[l].reshape(1, D_MODEL), w_in[l].astype(BF16))
        proj_dec = proj[n_prompt:].reshape(dec_batch, dec_seq, D_IN)
        o_hg_p, s_p = _hgrn_prompt(proj, lb, gn, batch, seq)
        o_mb_p = _moba_prompt(proj, batch, seq)
        o_hg_s, s_s = _hgrn_decode(proj, n_prompt, lb, gn, state_hgrn, l, dec_batch, dec_seq)
        o_mb_s = _moba_decode(proj_dec, cache_k, cache_v, page_table, l)
        o_hg = jnp.concatenate([o_hg_p, o_hg_s], axis=0)
        o_mb = jnp.concatenate([o_mb_p, o_mb_s.reshape(dec_batch * dec_seq, MB_WIDTH).astype(BF16)], axis=0)
        x = _outproj_ffn(x, o_hg, o_mb, w_out[l].astype(BF16), norm_ffn2[l].reshape(1, D_MODEL),
                         ffn2_wi[l].astype(BF16), ffn2_wo[l].astype(BF16), fn, final=(l == depth - 1))
        kp.append(proj[:n_prompt, k0:k0 + MB_WIDTH].reshape(batch, seq, MB_HEADS, MB_DH))
        vp.append(proj[:n_prompt, v0:v0 + MB_WIDTH].reshape(batch, seq, MB_HEADS, MB_DH))
        ks.append(proj_dec[:, :, k0:k0 + MB_WIDTH].reshape(dec_batch, dec_seq, MB_HEADS, MB_DH))
        vs.append(proj_dec[:, :, v0:v0 + MB_WIDTH].reshape(dec_batch, dec_seq, MB_HEADS, MB_DH))
        sp.append(s_p)
        ss.append(s_s)
    y_prompt = x[:n_prompt].reshape(batch, seq, D_MODEL)
    y_sample = x[n_prompt:].reshape(dec_batch, dec_seq, D_MODEL)
    return (y_prompt, y_sample, jnp.stack(kp), jnp.stack(vp), jnp.stack(sp), jnp.stack(ks), jnp.stack(vs),
            jnp.stack(ss))
```

```python
import functools

import jax
import jax.numpy as jnp
from jax import lax
from jax.experimental import pallas as pl
from jax.experimental.pallas import tpu as pltpu

F32 = jnp.float32
BF16 = jnp.bfloat16

D_MODEL = 1024
HG_HEADS = 4
HG_D = 128
HG_WIDTH = HG_HEADS * HG_D
MB_HEADS = 8
MB_DH = 64
MB_WIDTH = MB_HEADS * MB_DH
MB_BLOCK = 256
MB_TOPK = 3
D_FF = 2816
D_IN = 4 * HG_WIDTH + 3 * MB_WIDTH
D_MAIN = 4 * HG_WIDTH + 2 * MB_WIDTH
PAGE_SIZE = 128
EPS = 1e-6
FFN_SCALE = 0.5

LANES = 128
FF_CHUNK = 1408
DENSE_ROWS = 256
HG_CHUNK = 64
HG_ROWS = 512
HG_DEC_SEQS = 8
MB_GROUP = 2
MASKED = -(2.0 ** 30)
VMEM_LIMIT = 56 * 1024 * 1024

NEG_INF = float("-inf")


def _resident(shape):
    return pl.BlockSpec(shape, lambda *_: (0,) * len(shape), pipeline_mode=pl.Buffered(1))


def _params(semantics):
    return pltpu.CompilerParams(dimension_semantics=semantics, vmem_limit_bytes=VMEM_LIMIT)


def _rmsnorm(x, g):
    return x * lax.rsqrt(jnp.mean(x * x, axis=-1, keepdims=True) + EPS) * g


def _silu(x):
    return x * jax.nn.sigmoid(x)


def _dot_nt(a, b, **kw):
    return lax.dot_general(a, b, (((1,), (1,)), ((), ())), preferred_element_type=F32, **kw)


def _dot_tn(a, b, **kw):
    return lax.dot_general(a, b, (((0,), (0,)), ((), ())), preferred_element_type=F32, **kw)


def _swiglu_half_step(x, n_ref, wi_ref, wo_ref):
    h = _rmsnorm(x, n_ref[...]).astype(BF16)
    acc = jnp.zeros_like(x)
    for c in range(D_FF // FF_CHUNK):
        lo = c * FF_CHUNK
        a = jnp.dot(h, wi_ref[:, lo:lo + FF_CHUNK], preferred_element_type=F32)
        b = jnp.dot(h, wi_ref[:, D_FF + lo:D_FF + lo + FF_CHUNK], preferred_element_type=F32)
        g = (_silu(a) * b).astype(BF16)
        acc = acc + jnp.dot(g, wo_ref[lo:lo + FF_CHUNK, :], preferred_element_type=F32)
    return x + FFN_SCALE * acc


def _ffn_inproj_kernel(x_ref, n1_ref, wi_ref, wo_ref, nm_ref, win_ref, x1_ref, proj_ref):
    x1 = _swiglu_half_step(x_ref[...], n1_ref, wi_ref, wo_ref)
    x1_ref[...] = x1
    h = _rmsnorm(x1, nm_ref[...]).astype(BF16)
    proj_ref[...] = jnp.dot(h, win_ref[...], preferred_element_type=F32)


def _ffn_inproj(x, n1, wi, wo, nm, w_in):
    n = x.shape[0]
    rows = pl.BlockSpec((DENSE_ROWS, D_MODEL), lambda i: (i, 0))
    return pl.pallas_call(
        _ffn_inproj_kernel,
        grid=(n // DENSE_ROWS,),
        in_specs=[rows, _resident((1, D_MODEL)), _resident((D_MODEL, 2 * D_FF)), _resident((D_FF, D_MODEL)),
                  _resident((1, D_MODEL)), _resident((D_MODEL, D_IN))],
        out_specs=[rows, pl.BlockSpec((DENSE_ROWS, D_IN), lambda i: (i, 0))],
        out_shape=[jax.ShapeDtypeStruct((n, D_MODEL), F32), jax.ShapeDtypeStruct((n, D_IN), F32)],
        compiler_params=_params(("parallel",)),
        name="ffn_inproj",
    )(x, n1, wi, wo, nm, w_in)


def _ffn_inproj_prompt_kernel(x_ref, n1_ref, wi_ref, wo_ref, nm_ref, wmain_ref, wkt_ref, wvt_ref,
                              x1_ref, main_ref, kt_ref, vt_ref):
    x1 = _swiglu_half_step(x_ref[...], n1_ref, wi_ref, wo_ref)
    x1_ref[...] = x1
    h = _rmsnorm(x1, nm_ref[...]).astype(BF16)
    main_ref[...] = jnp.dot(h, wmain_ref[...], preferred_element_type=F32)
    kt_ref[0] = _dot_nt(wkt_ref[...], h)
    vt_ref[0] = _dot_nt(wvt_ref[...], h)


def _ffn_inproj_prompt(x, n1, wi, wo, nm, w_main, w_kt, w_vt, batch, seq):
    steps = seq // DENSE_ROWS
    rows = pl.BlockSpec((DENSE_ROWS, D_MODEL), lambda b, i: (b * steps + i, 0))
    tr = pl.BlockSpec((1, MB_WIDTH, DENSE_ROWS), lambda b, i: (b, 0, i))
    return pl.pallas_call(
        _ffn_inproj_prompt_kernel,
        grid=(batch, steps),
        in_specs=[rows, _resident((1, D_MODEL)), _resident((D_MODEL, 2 * D_FF)), _resident((D_FF, D_MODEL)),
                  _resident((1, D_MODEL)), _resident((D_MODEL, D_MAIN)), _resident((MB_WIDTH, D_MODEL)),
                  _resident((MB_WIDTH, D_MODEL))],
        out_specs=[rows, pl.BlockSpec((DENSE_ROWS, D_MAIN), lambda b, i: (b * steps + i, 0)), tr, tr],
        out_shape=[jax.ShapeDtypeStruct((batch * seq, D_MODEL), F32), jax.ShapeDtypeStruct((batch * seq, D_MAIN), F32),
                   jax.ShapeDtypeStruct((batch, MB_WIDTH, seq), F32), jax.ShapeDtypeStruct((batch, MB_WIDTH, seq), F32)],
        compiler_params=_params(("parallel", "parallel")),
        name="ffn_inproj_prompt",
    )(x, n1, wi, wo, nm, w_main, w_kt, w_vt)


def _outproj_ffn_kernel(x_ref, ohg_ref, omb_ref, wout_ref, n2_ref, wi_ref, wo_ref, fn_ref, y_ref, *, final):
    x = x_ref[...]
    x = x + jnp.dot(ohg_ref[...], wout_ref[:HG_WIDTH, :], preferred_element_type=F32)
    x = x + jnp.dot(omb_ref[...], wout_ref[HG_WIDTH:, :], preferred_element_type=F32)
    x = _swiglu_half_step(x, n2_ref, wi_ref, wo_ref)
    if final:
        x = _rmsnorm(x, fn_ref[...])
    y_ref[...] = x


def _outproj_ffn(x, o_hg, o_mb, w_out, n2, wi, wo, fn, final):
    n = x.shape[0]
    rows = pl.BlockSpec((DENSE_ROWS, D_MODEL), lambda i: (i, 0))
    half = pl.BlockSpec((DENSE_ROWS, HG_WIDTH), lambda i: (i, 0))
    return pl.pallas_call(
        functools.partial(_outproj_ffn_kernel, final=final),
        grid=(n // DENSE_ROWS,),
        in_specs=[rows, half, half, _resident((D_MODEL, D_MODEL)), _resident((1, D_MODEL)),
                  _resident((D_MODEL, 2 * D_FF)), _resident((D_FF, D_MODEL)), _resident((1, D_MODEL))],
        out_specs=rows,
        out_shape=jax.ShapeDtypeStruct((n, D_MODEL), F32),
        compiler_params=_params(("parallel",)),
        name="outproj_ffn",
    )(x, o_hg, o_mb, w_out, n2, wi, wo, fn)


def _segment_products(f, top):
    rows = f.shape[0]
    t = lax.broadcasted_iota(jnp.int32, f.shape, 0)
    gq, gk, tot = f, jnp.ones_like(f), f
    out = {1: (gq, gk)}
    m = 1
    while m < top:
        second = (t & m) != 0
        up = pltpu.roll(tot, m, 0)
        dn = pltpu.roll(tot, rows - m, 0)
        gq = jnp.where(second, gq * up, gq)
        gk = jnp.where(second, gk, gk * dn)
        tot = tot * jnp.where(second, up, dn)
        m *= 2
        out[m] = (gq, gk)
    return out


def _intra_scores(q, k, prods, top):
    rows = q.shape[0]
    t = lax.broadcasted_iota(jnp.int32, (rows, rows), 0)
    s = lax.broadcasted_iota(jnp.int32, (rows, rows), 1)
    ts = t ^ s
    a = jnp.where(t == s, _dot_nt(q.astype(BF16), k.astype(BF16)), 0.0)
    m = 1
    while m < top:
        gq, gk = prods[m]
        p = _dot_nt((q * gq).astype(BF16), (k * gk).astype(BF16))
        a = jnp.where((ts >= m) & (ts < 2 * m) & (t > s), p, a)
        m *= 2
    return a


def _hgrn_inputs(hq, z, lb):
    sig = jax.nn.sigmoid(z)
    f = lb + (1.0 - lb) * sig
    k = (1.0 - lb) * jax.nn.sigmoid(-z)
    return _silu(hq), k, f


def _hgrn_finish(o, hg, gn):
    return _rmsnorm(o, gn) * _silu(hg)


def _hgrn_prompt_kernel(hq_ref, hf_ref, hi_ref, hg_ref, lb_ref, gn_ref, o_ref, s_ref, st_ref):
    step = pl.program_id(1)

    @pl.when(step == 0)
    def _():
        st_ref[...] = jnp.zeros_like(st_ref)

    def chunk_body(j, carry):
        r0 = pl.multiple_of(j * HG_CHUNK, HG_CHUNK)
        rs = pl.ds(r0, HG_CHUNK)
        for h in range(HG_HEADS):
            cs = slice(h * HG_D, (h + 1) * HG_D)
            q, k, f = _hgrn_inputs(hq_ref[rs, cs], hf_ref[rs, cs], lb_ref[:, cs])
            v = hi_ref[rs, cs].astype(BF16)
            prods = _segment_products(f, HG_CHUNK)
            a = _intra_scores(q, k, prods, HG_CHUNK)
            gq, gk = prods[HG_CHUNK]
            st = st_ref[h]
            o = jnp.dot(a.astype(BF16), v, preferred_element_type=F32)
            o = o + _dot_nt((q * gq).astype(BF16), st.astype(BF16))
            st_ref[h] = st * gq[HG_CHUNK - 1:HG_CHUNK, :] + _dot_tn(v, (k * gk).astype(BF16))
            o_ref[rs, cs] = _hgrn_finish(o, hg_ref[rs, cs], gn_ref[...]).astype(o_ref.dtype)
        return carry

    lax.fori_loop(0, HG_ROWS // HG_CHUNK, chunk_body, 0)

    @pl.when(step == pl.num_programs(1) - 1)
    def _():
        for h in range(HG_HEADS):
            s_ref[0, h] = st_ref[h].T


def _hgrn_prompt(proj, lb, gn, batch, seq):
    steps = seq // HG_ROWS

    def col(c):
        return pl.BlockSpec((HG_ROWS, HG_WIDTH), lambda b, i: (b * steps + i, c))

    return pl.pallas_call(
        _hgrn_prompt_kernel,
        grid=(batch, steps),
        in_specs=[col(0), col(1), col(2), col(3), _resident((1, HG_WIDTH)), _resident((1, HG_D))],
        out_specs=[pl.BlockSpec((HG_ROWS, HG_WIDTH), lambda b, i: (b * steps + i, 0)),
                   pl.BlockSpec((1, HG_HEADS, HG_D, HG_D), lambda b, i: (b, 0, 0, 0))],
        out_shape=[jax.ShapeDtypeStruct((batch * seq, HG_WIDTH), BF16),
                   jax.ShapeDtypeStruct((batch, HG_HEADS, HG_D, HG_D), F32)],
        scratch_shapes=[pltpu.VMEM((HG_HEADS, HG_D, HG_D), F32)],
        compiler_params=_params(("parallel", "arbitrary")),
        name="hgrn_prompt",
    )(proj, proj, proj, proj, lb, gn)


def _hgrn_decode_kernel(hq_ref, hf_ref, hi_ref, hg_ref, lb_ref, gn_ref, s0_ref, o_ref, s_ref, *, seq):
    rows = HG_DEC_SEQS * seq
    row = lax.broadcasted_iota(jnp.int32, (rows, HG_D), 0)
    for h in range(HG_HEADS):
        cs = slice(h * HG_D, (h + 1) * HG_D)
        q, k, f = _hgrn_inputs(hq_ref[:, cs], hf_ref[:, cs], lb_ref[:, cs])
        v = hi_ref[:, cs].astype(BF16)
        prods = _segment_products(f, seq)
        a = _intra_scores(q, k, prods, seq)
        gq, gk = prods[seq]
        qg, kg = q * gq, k * gk
        o = jnp.dot(a.astype(BF16), v, preferred_element_type=F32)
        for i in range(HG_DEC_SEQS):
            mine = (row >= i * seq) & (row < (i + 1) * seq)
            s0 = s0_ref[0, i, h]
            o = o + jnp.dot(jnp.where(mine, qg, 0.0).astype(BF16), s0.astype(BF16), preferred_element_type=F32)
            last = (i + 1) * seq - 1
            decay = jnp.broadcast_to(gq[last:last + 1, :], (HG_D, HG_D)).T
            s_ref[i, h] = s0 * decay + _dot_tn(jnp.where(mine, kg, 0.0).astype(BF16), v)
        o_ref[:, cs] = _hgrn_finish(o, hg_ref[:, cs], gn_ref[...]).astype(o_ref.dtype)


def _hgrn_decode(proj, lb, gn, state, layer, batch, seq):
    rows = HG_DEC_SEQS * seq

    def col(c):
        return pl.BlockSpec((rows, HG_WIDTH), lambda g: (g, c))

    return pl.pallas_call(
        functools.partial(_hgrn_decode_kernel, seq=seq),
        grid=(batch // HG_DEC_SEQS,),
        in_specs=[col(0), col(1), col(2), col(3), _resident((1, HG_WIDTH)), _resident((1, HG_D)),
                  pl.BlockSpec((1, HG_DEC_SEQS, HG_HEADS, HG_D, HG_D), lambda g: (layer, g, 0, 0, 0))],
        out_specs=[pl.BlockSpec((rows, HG_WIDTH), lambda g: (g, 0)),
                   pl.BlockSpec((HG_DEC_SEQS, HG_HEADS, HG_D, HG_D), lambda g: (g, 0, 0, 0))],
        out_shape=[jax.ShapeDtypeStruct((batch * seq, HG_WIDTH), BF16),
                   jax.ShapeDtypeStruct((batch, HG_HEADS, HG_D, HG_D), F32)],
        compiler_params=_params(("parallel",)),
        name="hgrn_decode",
    )(proj, proj, proj, proj, lb, gn, state)


def _top_blocks(gate, valid):
    lanef = lax.broadcasted_iota(jnp.int32, gate.shape, 1).astype(F32)
    g = jnp.where(valid, gate, NEG_INF)
    picks = []
    for _ in range(MB_TOPK):
        mx = jnp.max(g, axis=-1, keepdims=True)
        idx = jnp.min(jnp.where(g == mx, lanef, float(LANES)), axis=-1, keepdims=True)
        picks.append(jnp.where(mx > NEG_INF, idx, -1.0))
        g = jnp.where(lanef == idx, NEG_INF, g)
    return picks


def _picked_offsets_by_row(gate_t, n_valid):
    rowi = lax.broadcasted_iota(jnp.int32, gate_t.shape, 0)
    rowf = rowi.astype(F32)
    g = jnp.where(rowi < n_valid, gate_t, NEG_INF)
    offsets = jnp.full(gate_t.shape, MASKED, F32)
    for _ in range(MB_TOPK):
        mx = jnp.max(g, axis=0, keepdims=True)
        idx = jnp.min(jnp.where(g == mx, rowf, float(LANES)), axis=0, keepdims=True)
        hit = (rowf == idx) & (mx > NEG_INF)
        offsets = jnp.where(hit, 0.0, offsets)
        g = jnp.where(hit, NEG_INF, g)
    return offsets


def _moba_prompt_kernel(q_ref, k_ref, vt_ref, o_ref, ka_ref, kb_ref, va_ref, vb_ref, kma_ref, kmb_ref, *, n_blocks):
    i = pl.program_id(2)
    lane = lax.broadcasted_iota(jnp.int32, (MB_BLOCK, LANES), 1)
    first = lane < MB_DH
    feat = lax.broadcasted_iota(jnp.int32, (LANES, MB_BLOCK), 0)
    top = feat < MB_DH

    @pl.when(i == 0)
    def _():
        kma_ref[...] = jnp.zeros_like(kma_ref)
        kmb_ref[...] = jnp.zeros_like(kmb_ref)

        def prep(j, carry):
            rs = pl.ds(pl.multiple_of(j * MB_BLOCK, MB_BLOCK), MB_BLOCK)
            k = k_ref[rs, :]
            vt = vt_ref[0, :, rs]
            ka_ref[rs, :] = jnp.where(first, k, jnp.where(lane == MB_DH + j, 1.0, 0.0)).astype(BF16)
            kb_ref[rs, :] = jnp.where(first, jnp.where(lane == j, 1.0, 0.0), k).astype(BF16)
            va_ref[:, rs] = jnp.where(top, vt, 1.0).astype(BF16)
            vb_ref[:, rs] = jnp.where(top, 1.0, vt).astype(BF16)
            km = jnp.mean(k, axis=0, keepdims=True)
            kma_ref[pl.ds(MB_DH + j, 1), :] = km
            kmb_ref[pl.ds(j, 1), :] = km
            return carry

        lax.fori_loop(0, n_blocks, prep, 0)

    q = q_ref[...] * (MB_DH ** -0.5)
    zeros = jnp.zeros((MB_DH, MB_BLOCK), F32)
    heads = []
    for is_a in (True, False):
        qh = jnp.where(first, q, 0.0) if is_a else jnp.where(first, 0.0, q)
        km_ref, k_ref_h, v_ref_h = (kma_ref, ka_ref, va_ref) if is_a else (kmb_ref, kb_ref, vb_ref)
        gate_t = _dot_nt(km_ref[...], qh, precision=lax.Precision.HIGHEST)
        offs = _picked_offsets_by_row(gate_t[MB_DH:] if is_a else gate_t[:MB_DH], i)
        sel = jnp.concatenate([zeros, offs] if is_a else [offs, zeros], axis=0).T
        q_sel = (jnp.where(first, q, sel) if is_a else jnp.where(first, sel, q)).astype(BF16)
        heads.append((qh.astype(BF16), q_sel, k_ref_h, v_ref_h))

    own = pl.ds(pl.multiple_of(i * MB_BLOCK, MB_BLOCK), MB_BLOCK)
    key = lax.broadcasted_iota(jnp.int32, (MB_BLOCK, MB_BLOCK), 0)
    qry = lax.broadcasted_iota(jnp.int32, (MB_BLOCK, MB_BLOCK), 1)
    init = []
    for q_own, _, k_ref_h, v_ref_h in heads:
        s = jnp.where(key <= qry, _dot_nt(k_ref_h[own, :], q_own), NEG_INF)
        m = jnp.max(s, axis=0, keepdims=True)
        init += [m, jnp.dot(v_ref_h[:, own], jnp.exp(s - m).astype(BF16), preferred_element_type=F32)]

    n_groups = (i + MB_GROUP - 1) // MB_GROUP

    def group_scores(g):
        rs = pl.ds(pl.multiple_of(g * (MB_GROUP * MB_BLOCK), MB_GROUP * MB_BLOCK), MB_GROUP * MB_BLOCK)
        return [_dot_nt(k_ref_h[rs, :], q_sel) for _, q_sel, k_ref_h, _ in heads]

    def past_group(g, carry):
        nxt = group_scores(jnp.minimum(g + 1, n_groups - 1))
        rs = pl.ds(pl.multiple_of(g * (MB_GROUP * MB_BLOCK), MB_GROUP * MB_BLOCK), MB_GROUP * MB_BLOCK)
        out = []
        for h, (_, _, _, v_ref_h) in enumerate(heads):
            s, m, acc = carry[3 * h:3 * h + 3]
            m_new = jnp.maximum(m, jnp.max(s, axis=0, keepdims=True))
            p = jnp.exp(s - m_new).astype(BF16)
            out += [nxt[h], m_new, jnp.exp(m - m_new) * acc + jnp.dot(v_ref_h[:, rs], p, preferred_element_type=F32)]
        return tuple(out)

    s0 = group_scores(0)
    start = (s0[0], init[0], init[1], s0[1], init[2], init[3])
    _, _, acc_a, _, _, acc_b = lax.fori_loop(0, n_groups, past_group, start)
    num = jnp.where(top, acc_a, acc_b)
    den = jnp.where(top, acc_a[MB_DH:MB_DH + 1, :], acc_b[:1, :])
    o_ref[...] = (num / den).T.astype(o_ref.dtype)


def _moba_prompt(main, vt, batch, seq):
    n_blocks = seq // MB_BLOCK
    assert n_blocks % MB_GROUP == 0 and n_blocks <= MB_DH
    pairs = MB_WIDTH // LANES
    q0 = 4 * HG_WIDTH // LANES
    return pl.pallas_call(
        functools.partial(_moba_prompt_kernel, n_blocks=n_blocks),
        grid=(batch, pairs, n_blocks),
        in_specs=[pl.BlockSpec((MB_BLOCK, LANES), lambda b, p, i: (b * n_blocks + i, q0 + p)),
                  pl.BlockSpec((seq, LANES), lambda b, p, i: (b, q0 + pairs + p)),
                  pl.BlockSpec((1, LANES, seq), lambda b, p, i: (b, p, 0))],
        out_specs=pl.BlockSpec((MB_BLOCK, LANES), lambda b, p, i: (b * n_blocks + i, p)),
        out_shape=jax.ShapeDtypeStruct((batch * seq, MB_WIDTH), BF16),
        scratch_shapes=[pltpu.VMEM((seq, LANES), BF16)] * 2 + [pltpu.VMEM((LANES, seq), BF16)] * 2
        + [pltpu.VMEM((LANES, LANES), F32)] * 2,
        compiler_params=_params(("parallel", "parallel", "arbitrary")),
        name="moba_prompt",
    )(main, main, vt)


def _moba_decode_kernel(pt_ref, q_ref, k_ref, v_ref, *refs, n_pages, seq):
    del pt_ref
    k_pages, v_pages, o_ref = refs[:n_pages], refs[n_pages:2 * n_pages], refs[2 * n_pages]
    pages_per_block = MB_BLOCK // PAGE_SIZE
    n_blocks = n_pages // pages_per_block
    rows = seq * MB_HEADS
    sub = lax.broadcasted_iota(jnp.int32, (MB_HEADS, MB_WIDTH), 0)
    head_of_lane = lax.broadcasted_iota(jnp.int32, (MB_HEADS, MB_WIDTH), 1) // MB_DH
    head_mask = sub == head_of_lane
    big = lax.broadcasted_iota(jnp.int32, (LANES, MB_WIDTH), 0)

    def rows_to_tile(x):
        out = jnp.zeros((LANES, MB_WIDTH), F32)
        for t in range(x.shape[0]):
            out = jnp.where(big == t, jnp.broadcast_to(x[t:t + 1, :], out.shape), out)
        return out

    q = q_ref[0] * (MB_DH ** -0.5)
    qbd = jnp.concatenate(
        [jnp.where(head_mask, jnp.broadcast_to(q[t:t + 1, :], head_mask.shape), 0.0) for t in range(seq)], axis=0)
    qbd16 = qbd.astype(BF16)

    lane = lax.broadcasted_iota(jnp.int32, (MB_WIDTH, LANES), 1)
    kmean_t = jnp.zeros((MB_WIDTH, LANES), F32)
    for j in range(n_blocks):
        tot = sum(k_pages[j * pages_per_block + u][0] for u in range(pages_per_block))
        kmean_t = jnp.where(lane == j, jnp.sum(tot, axis=1, keepdims=True) * (1.0 / MB_BLOCK), kmean_t)
    gate = jnp.dot(qbd, kmean_t, preferred_element_type=F32, precision=lax.Precision.HIGHEST)
    glane = lax.broadcasted_iota(jnp.int32, (rows, LANES), 1)
    picks = _top_blocks(gate, glane < n_blocks)

    def block_bias(j):
        hit = (picks[0] == j) | (picks[1] == j) | (picks[2] == j)
        return jnp.where(hit, 0.0, NEG_INF)

    scores = [jnp.dot(qbd16, k_pages[p][0].astype(BF16), preferred_element_type=F32)
              + block_bias(float(p // pages_per_block)) for p in range(n_pages)]
    r = lax.broadcasted_iota(jnp.int32, (rows, LANES), 0) // MB_HEADS
    scores.append(jnp.where(glane <= r, _dot_nt(qbd16, rows_to_tile(k_ref[0]).astype(BF16)), NEG_INF))

    m = functools.reduce(jnp.maximum, scores)
    m = jnp.max(m, axis=-1, keepdims=True)
    l = jnp.zeros((rows, 1), F32)
    acc = jnp.zeros((rows, MB_WIDTH), F32)
    for p, s in enumerate(scores):
        w = jnp.exp(s - m)
        l = l + jnp.sum(w, axis=-1, keepdims=True)
        if p < n_pages:
            acc = acc + _dot_nt(w.astype(BF16), v_pages[p][0].astype(BF16))
        else:
            acc = acc + jnp.dot(w.astype(BF16), rows_to_tile(v_ref[0]).astype(BF16), preferred_element_type=F32)
    acc = acc / l
    for t in range(seq):
        own = jnp.where(head_mask, acc[t * MB_HEADS:(t + 1) * MB_HEADS, :], 0.0)
        o_ref[0, t:t + 1, :] = jnp.sum(own, axis=0, keepdims=True)


def _moba_decode(proj_dec, cache_kt, cache_vt, page_table, layer, n_pool):
    batch, seq, _ = proj_dec.shape
    n_pages = page_table.shape[1]
    q0 = 4 * HG_WIDTH // MB_WIDTH

    def new(c):
        return pl.BlockSpec((1, seq, MB_WIDTH), lambda b, pt: (b, 0, q0 + c))

    def page(j):
        return pl.BlockSpec((1, MB_WIDTH, PAGE_SIZE), lambda b, pt: (layer * n_pool + pt[b * n_pages + j], 0, 0))

    pages = [page(j) for j in range(n_pages)]
    grid_spec = pltpu.PrefetchScalarGridSpec(
        num_scalar_prefetch=1,
        grid=(batch,),
        in_specs=[new(0), new(1), new(2)] + pages + pages,
        out_specs=pl.BlockSpec((1, seq, MB_WIDTH), lambda b, pt: (b, 0, 0)),
    )
    return pl.pallas_call(
        functools.partial(_moba_decode_kernel, n_pages=n_pages, seq=seq),
        grid_spec=grid_spec,
        out_shape=jax.ShapeDtypeStruct((batch, seq, MB_WIDTH), F32),
        compiler_params=_params(("parallel",)),
        name="moba_decode",
    )(page_table.reshape(-1), proj_dec, proj_dec, proj_dec, *([cache_kt] * n_pages), *([cache_vt] * n_pages))


def _lower_bounds(gamma):
    p = jax.nn.softmax(gamma.astype(F32), axis=0)
    return jnp.cumsum(p, axis=0) - p[:1]


def _tokens_last_to_heads(x_t, batch, seq):
    return jnp.transpose(x_t.reshape(batch, MB_HEADS, MB_DH, seq), (0, 3, 1, 2))


def kernel(x_prompt, x_sample, cache_k, cache_v, state_hgrn, page_table, hg_lower_bounds, w_in, w_out, hg_gnorm,
           norm_ffn1, ffn1_wi, ffn1_wo, norm_mix, norm_ffn2, ffn2_wi, ffn2_wo, final_norm):
    batch, seq, _ = x_prompt.shape
    dec_batch, dec_seq, _ = x_sample.shape
    depth, n_pool = cache_k.shape[:2]
    lower = _lower_bounds(hg_lower_bounds)
    xp = x_prompt.reshape(batch * seq, D_MODEL)
    xs = x_sample.reshape(dec_batch * dec_seq, D_MODEL)
    fn = final_norm.reshape(1, D_MODEL)
    cache_kt = jnp.transpose(cache_k, (0, 1, 3, 4, 2)).reshape(depth * n_pool, MB_WIDTH, PAGE_SIZE)
    cache_vt = jnp.transpose(cache_v, (0, 1, 3, 4, 2)).reshape(depth * n_pool, MB_WIDTH, PAGE_SIZE)
    kp, vp, sp, ks, vs, ss = [], [], [], [], [], []
    k0, v0 = D_IN - 2 * MB_WIDTH, D_IN - MB_WIDTH
    for l in range(depth):
        lb = lower[l].reshape(1, HG_WIDTH)
        gn = hg_gnorm[l].reshape(1, HG_D)
        n1, nm, n2 = (a[l].reshape(1, D_MODEL) for a in (norm_ffn1, norm_mix, norm_ffn2))
        wi1, wo1, wi2, wo2 = (a[l].astype(BF16) for a in (ffn1_wi, ffn1_wo, ffn2_wi, ffn2_wo))
        w_in_l, w_out_l = w_in[l].astype(BF16), w_out[l].astype(BF16)
        w_kt, w_vt = w_in_l[:, k0:v0].T, w_in_l[:, v0:].T

        xp, main, kt, vt = _ffn_inproj_prompt(xp, n1, wi1, wo1, nm, w_in_l[:, :D_MAIN], w_kt, w_vt, batch, seq)
        o_hg, s_p = _hgrn_prompt(main, lb, gn, batch, seq)
        o_mb = _moba_prompt(main, vt, batch, seq)
        xp = _outproj_ffn(xp, o_hg, o_mb, w_out_l, n2, wi2, wo2, fn, final=(l == depth - 1))
        kp.append(_tokens_last_to_heads(kt, batch, seq))
        vp.append(_tokens_last_to_heads(vt, batch, seq))
        sp.append(s_p)

        xs, proj = _ffn_inproj(xs, n1, wi1, wo1, nm, w_in_l)
        proj_dec = proj.reshape(dec_batch, dec_seq, D_IN)
        o_hg, s_s = _hgrn_decode(proj, lb, gn, state_hgrn, l, dec_batch, dec_seq)
        o_mb = _moba_decode(proj_dec, cache_kt, cache_vt, page_table, l, n_pool)
        o_mb = o_mb.reshape(dec_batch * dec_seq, MB_WIDTH).astype(BF16)
        xs = _outproj_ffn(xs, o_hg, o_mb, w_out_l, n2, wi2, wo2, fn, final=(l == depth - 1))
        ks.append(proj_dec[:, :, k0:v0].reshape(dec_batch, dec_seq, MB_HEADS, MB_DH))
        vs.append(proj_dec[:, :, v0:].reshape(dec_batch, dec_seq, MB_HEADS, MB_DH))
        ss.append(s_s)
    return (xp.reshape(batch, seq, D_MODEL), xs.reshape(dec_batch, dec_seq, D_MODEL), jnp.stack(kp), jnp.stack(vp),
            jnp.stack(sp), jnp.stack(ks), jnp.stack(vs), jnp.stack(ss))
```

```python
import functools

import jax
import jax.numpy as jnp
from jax import lax
from jax.experimental import pallas as pl
from jax.experimental.pallas import tpu as pltpu

F32 = jnp.float32
BF16 = jnp.bfloat16

D_MODEL = 1024
HG_HEADS = 4
HG_D = 128
HG_WIDTH = HG_HEADS * HG_D
MB_HEADS = 8
MB_DH = 64
MB_WIDTH = MB_HEADS * MB_DH
MB_BLOCK = 256
MB_TOPK = 3
D_FF = 2816
D_IN = 4 * HG_WIDTH + 3 * MB_WIDTH
D_MAIN = 4 * HG_WIDTH + 2 * MB_WIDTH
PAGE_SIZE = 128
EPS = 1e-6
FFN_SCALE = 0.5

LANES = 128
FF_CHUNK = 1408
DENSE_ROWS = 256
HG_CHUNK = 64
HG_ROWS = 512
HG_DEC_SEQS = 8
MB_GROUP = 2
MASKED = -(2.0 ** 30)
VMEM_LIMIT = 56 * 1024 * 1024

NEG_INF = float("-inf")


def _resident(shape):
    return pl.BlockSpec(shape, lambda *_: (0,) * len(shape), pipeline_mode=pl.Buffered(1))


def _params(semantics):
    return pltpu.CompilerParams(dimension_semantics=semantics, vmem_limit_bytes=VMEM_LIMIT)


def _rmsnorm(x, g):
    return x * lax.rsqrt(jnp.mean(x * x, axis=-1, keepdims=True) + EPS) * g


def _silu(x):
    return x * jax.nn.sigmoid(x)


def _dot_nt(a, b, **kw):
    return lax.dot_general(a, b, (((1,), (1,)), ((), ())), preferred_element_type=F32, **kw)


def _dot_tn(a, b, **kw):
    return lax.dot_general(a, b, (((0,), (0,)), ((), ())), preferred_element_type=F32, **kw)


def _swiglu_half_step(x, n_ref, wi_ref, wo_ref):
    h = _rmsnorm(x, n_ref[...]).astype(BF16)
    acc = jnp.zeros_like(x)
    for c in range(D_FF // FF_CHUNK):
        lo = c * FF_CHUNK
        a = jnp.dot(h, wi_ref[:, lo:lo + FF_CHUNK], preferred_element_type=F32)
        b = jnp.dot(h, wi_ref[:, D_FF + lo:D_FF + lo + FF_CHUNK], preferred_element_type=F32)
        g = (_silu(a) * b).astype(BF16)
        acc = acc + jnp.dot(g, wo_ref[lo:lo + FF_CHUNK, :], preferred_element_type=F32)
    return x + FFN_SCALE * acc


def _ffn_inproj_kernel(x_ref, n1_ref, wi_ref, wo_ref, nm_ref, win_ref, x1_ref, proj_ref):
    x1 = _swiglu_half_step(x_ref[...], n1_ref, wi_ref, wo_ref)
    x1_ref[...] = x1
    h = _rmsnorm(x1, nm_ref[...]).astype(BF16)
    proj_ref[...] = jnp.dot(h, win_ref[...], preferred_element_type=F32)


def _ffn_inproj(x, n1, wi, wo, nm, w_in):
    n = x.shape[0]
    rows = pl.BlockSpec((DENSE_ROWS, D_MODEL), lambda i: (i, 0))
    return pl.pallas_call(
        _ffn_inproj_kernel,
        grid=(n // DENSE_ROWS,),
        in_specs=[rows, _resident((1, D_MODEL)), _resident((D_MODEL, 2 * D_FF)), _resident((D_FF, D_MODEL)),
                  _resident((1, D_MODEL)), _resident((D_MODEL, D_IN))],
        out_specs=[rows, pl.BlockSpec((DENSE_ROWS, D_IN), lambda i: (i, 0))],
        out_shape=[jax.ShapeDtypeStruct((n, D_MODEL), F32), jax.ShapeDtypeStruct((n, D_IN), F32)],
        compiler_params=_params(("parallel",)),
        name="ffn_inproj",
    )(x, n1, wi, wo, nm, w_in)


def _ffn_inproj_prompt_kernel(x_ref, n1_ref, wi_ref, wo_ref, nm_ref, wmain_ref, wvt_ref, x1_ref, main_ref, kt_ref, vt_ref):
    x1 = _swiglu_half_step(x_ref[...], n1_ref, wi_ref, wo_ref)
    x1_ref[...] = x1
    h = _rmsnorm(x1, nm_ref[...]).astype(BF16)
    main = jnp.dot(h, wmain_ref[...], preferred_element_type=F32)
    main_ref[...] = main
    kt_ref[0] = main[:, D_MAIN - MB_WIDTH:].T
    vt_ref[0] = _dot_nt(wvt_ref[...], h)


def _ffn_inproj_prompt(x, n1, wi, wo, nm, w_main, w_vt, batch, seq):
    steps = seq // DENSE_ROWS
    rows = pl.BlockSpec((DENSE_ROWS, D_MODEL), lambda b, i: (b * steps + i, 0))
    tr = pl.BlockSpec((1, MB_WIDTH, DENSE_ROWS), lambda b, i: (b, 0, i))
    return pl.pallas_call(
        _ffn_inproj_prompt_kernel,
        grid=(batch, steps),
        in_specs=[rows, _resident((1, D_MODEL)), _resident((D_MODEL, 2 * D_FF)), _resident((D_FF, D_MODEL)),
                  _resident((1, D_MODEL)), _resident((D_MODEL, D_MAIN)), _resident((MB_WIDTH, D_MODEL))],
        out_specs=[rows, pl.BlockSpec((DENSE_ROWS, D_MAIN), lambda b, i: (b * steps + i, 0)), tr, tr],
        out_shape=[jax.ShapeDtypeStruct((batch * seq, D_MODEL), F32), jax.ShapeDtypeStruct((batch * seq, D_MAIN), F32),
                   jax.ShapeDtypeStruct((batch, MB_WIDTH, seq), F32), jax.ShapeDtypeStruct((batch, MB_WIDTH, seq), F32)],
        compiler_params=_params(("parallel", "parallel")),
        name="ffn_inproj_prompt",
    )(x, n1, wi, wo, nm, w_main, w_vt)


def _outproj_ffn_kernel(x_ref, ohg_ref, omb_ref, wout_ref, n2_ref, wi_ref, wo_ref, fn_ref, y_ref, *, final):
    x = x_ref[...]
    x = x + jnp.dot(ohg_ref[...], wout_ref[:HG_WIDTH, :], preferred_element_type=F32)
    x = x + jnp.dot(omb_ref[...], wout_ref[HG_WIDTH:, :], preferred_element_type=F32)
    x = _swiglu_half_step(x, n2_ref, wi_ref, wo_ref)
    if final:
        x = _rmsnorm(x, fn_ref[...])
    y_ref[...] = x


def _outproj_ffn(x, o_hg, o_mb, w_out, n2, wi, wo, fn, final):
    n = x.shape[0]
    rows = pl.BlockSpec((DENSE_ROWS, D_MODEL), lambda i: (i, 0))
    half = pl.BlockSpec((DENSE_ROWS, HG_WIDTH), lambda i: (i, 0))
    return pl.pallas_call(
        functools.partial(_outproj_ffn_kernel, final=final),
        grid=(n // DENSE_ROWS,),
        in_specs=[rows, half, half, _resident((D_MODEL, D_MODEL)), _resident((1, D_MODEL)),
                  _resident((D_MODEL, 2 * D_FF)), _resident((D_FF, D_MODEL)), _resident((1, D_MODEL))],
        out_specs=rows,
        out_shape=jax.ShapeDtypeStruct((n, D_MODEL), F32),
        compiler_params=_params(("parallel",)),
        name="outproj_ffn",
    )(x, o_hg, o_mb, w_out, n2, wi, wo, fn)


def _segment_products(f, top):
    rows = f.shape[0]
    t = lax.broadcasted_iota(jnp.int32, f.shape, 0)
    gq, gk, tot = f, jnp.ones_like(f), f
    out = {1: (gq, gk)}
    m = 1
    while m < top:
        second = (t & m) != 0
        up = pltpu.roll(tot, m, 0)
        dn = pltpu.roll(tot, rows - m, 0)
        gq = jnp.where(second, gq * up, gq)
        gk = jnp.where(second, gk, gk * dn)
        tot = tot * jnp.where(second, up, dn)
        m *= 2
        out[m] = (gq, gk)
    return out


def _intra_scores(q, k, prods, top):
    rows = q.shape[0]
    t = lax.broadcasted_iota(jnp.int32, (rows, rows), 0)
    s = lax.broadcasted_iota(jnp.int32, (rows, rows), 1)
    ts = t ^ s
    a = jnp.where(t == s, _dot_nt(q.astype(BF16), k.astype(BF16)), 0.0)
    m = 1
    while m < top:
        gq, gk = prods[m]
        p = _dot_nt((q * gq).astype(BF16), (k * gk).astype(BF16))
        a = jnp.where((ts >= m) & (ts < 2 * m) & (t > s), p, a)
        m *= 2
    return a


def _hgrn_inputs(hq, z, lb):
    sig = jax.nn.sigmoid(z)
    f = lb + (1.0 - lb) * sig
    k = (1.0 - lb) * jax.nn.sigmoid(-z)
    return _silu(hq), k, f


def _hgrn_finish(o, hg, gn):
    return _rmsnorm(o, gn) * _silu(hg)


def _hgrn_prompt_kernel(hq_ref, hf_ref, hi_ref, hg_ref, lb_ref, gn_ref, o_ref, s_ref, st_ref):
    step = pl.program_id(1)

    @pl.when(step == 0)
    def _():
        st_ref[...] = jnp.zeros_like(st_ref)

    def chunk_body(j, carry):
        r0 = pl.multiple_of(j * HG_CHUNK, HG_CHUNK)
        rs = pl.ds(r0, HG_CHUNK)
        for h in range(HG_HEADS):
            cs = slice(h * HG_D, (h + 1) * HG_D)
            q, k, f = _hgrn_inputs(hq_ref[rs, cs], hf_ref[rs, cs], lb_ref[:, cs])
            v = hi_ref[rs, cs].astype(BF16)
            prods = _segment_products(f, HG_CHUNK)
            a = _intra_scores(q, k, prods, HG_CHUNK)
            gq, gk = prods[HG_CHUNK]
            st = st_ref[h]
            o = jnp.dot(a.astype(BF16), v, preferred_element_type=F32)
            o = o + _dot_nt((q * gq).astype(BF16), st.astype(BF16))
            st_ref[h] = st * gq[HG_CHUNK - 1:HG_CHUNK, :] + _dot_tn(v, (k * gk).astype(BF16))
            o_ref[rs, cs] = _hgrn_finish(o, hg_ref[rs, cs], gn_ref[...]).astype(o_ref.dtype)
        return carry

    lax.fori_loop(0, HG_ROWS // HG_CHUNK, chunk_body, 0)

    @pl.when(step == pl.num_programs(1) - 1)
    def _():
        for h in range(HG_HEADS):
            s_ref[0, h] = st_ref[h].T


def _hgrn_prompt(proj, lb, gn, batch, seq):
    steps = seq // HG_ROWS

    def col(c):
        return pl.BlockSpec((HG_ROWS, HG_WIDTH), lambda b, i: (b * steps + i, c))

    return pl.pallas_call(
        _hgrn_prompt_kernel,
        grid=(batch, steps),
        in_specs=[col(0), col(1), col(2), col(3), _resident((1, HG_WIDTH)), _resident((1, HG_D))],
        out_specs=[pl.BlockSpec((HG_ROWS, HG_WIDTH), lambda b, i: (b * steps + i, 0)),
                   pl.BlockSpec((1, HG_HEADS, HG_D, HG_D), lambda b, i: (b, 0, 0, 0))],
        out_shape=[jax.ShapeDtypeStruct((batch * seq, HG_WIDTH), BF16),
                   jax.ShapeDtypeStruct((batch, HG_HEADS, HG_D, HG_D), F32)],
        scratch_shapes=[pltpu.VMEM((HG_HEADS, HG_D, HG_D), F32)],
        compiler_params=_params(("parallel", "arbitrary")),
        name="hgrn_prompt",
    )(proj, proj, proj, proj, lb, gn)


def _hgrn_decode_kernel(hq_ref, hf_ref, hi_ref, hg_ref, lb_ref, gn_ref, s0_ref, o_ref, s_ref, *, seq):
    rows = HG_DEC_SEQS * seq
    row = lax.broadcasted_iota(jnp.int32, (rows, HG_D), 0)
    for h in range(HG_HEADS):
        cs = slice(h * HG_D, (h + 1) * HG_D)
        q, k, f = _hgrn_inputs(hq_ref[:, cs], hf_ref[:, cs], lb_ref[:, cs])
        v = hi_ref[:, cs].astype(BF16)
        prods = _segment_products(f, seq)
        a = _intra_scores(q, k, prods, seq)
        gq, gk = prods[seq]
        qg, kg = q * gq, k * gk
        o = jnp.dot(a.astype(BF16), v, preferred_element_type=F32)
        for i in range(HG_DEC_SEQS):
            mine = (row >= i * seq) & (row < (i + 1) * seq)
            s0 = s0_ref[0, i, h]
            o = o + jnp.dot(jnp.where(mine, qg, 0.0).astype(BF16), s0.astype(BF16), preferred_element_type=F32)
            last = (i + 1) * seq - 1
            decay = jnp.broadcast_to(gq[last:last + 1, :], (HG_D, HG_D)).T
            s_ref[i, h] = s0 * decay + _dot_tn(jnp.where(mine, kg, 0.0).astype(BF16), v)
        o_ref[:, cs] = _hgrn_finish(o, hg_ref[:, cs], gn_ref[...]).astype(o_ref.dtype)


def _hgrn_decode(proj, lb, gn, state, layer, batch, seq):
    rows = HG_DEC_SEQS * seq

    def col(c):
        return pl.BlockSpec((rows, HG_WIDTH), lambda g: (g, c))

    return pl.pallas_call(
        functools.partial(_hgrn_decode_kernel, seq=seq),
        grid=(batch // HG_DEC_SEQS,),
        in_specs=[col(0), col(1), col(2), col(3), _resident((1, HG_WIDTH)), _resident((1, HG_D)),
                  pl.BlockSpec((1, HG_DEC_SEQS, HG_HEADS, HG_D, HG_D), lambda g: (layer, g, 0, 0, 0))],
        out_specs=[pl.BlockSpec((rows, HG_WIDTH), lambda g: (g, 0)),
                   pl.BlockSpec((HG_DEC_SEQS, HG_HEADS, HG_D, HG_D), lambda g: (g, 0, 0, 0))],
        out_shape=[jax.ShapeDtypeStruct((batch * seq, HG_WIDTH), BF16),
                   jax.ShapeDtypeStruct((batch, HG_HEADS, HG_D, HG_D), F32)],
        compiler_params=_params(("parallel",)),
        name="hgrn_decode",
    )(proj, proj, proj, proj, lb, gn, state)


def _top_blocks(gate, valid):
    lanef = lax.broadcasted_iota(jnp.int32, gate.shape, 1).astype(F32)
    g = jnp.where(valid, gate, NEG_INF)
    picks = []
    for _ in range(MB_TOPK):
        mx = jnp.max(g, axis=-1, keepdims=True)
        idx = jnp.min(jnp.where(g == mx, lanef, float(LANES)), axis=-1, keepdims=True)
        picks.append(jnp.where(mx > NEG_INF, idx, -1.0))
        g = jnp.where(lanef == idx, NEG_INF, g)
    return picks


def _picked_offsets_by_row(gate_t, n_valid):
    rowi = lax.broadcasted_iota(jnp.int32, gate_t.shape, 0)
    rowf = rowi.astype(F32)
    g = jnp.where(rowi < n_valid, gate_t, NEG_INF)
    offsets = jnp.full(gate_t.shape, MASKED, F32)
    for _ in range(MB_TOPK):
        mx = jnp.max(g, axis=0, keepdims=True)
        idx = jnp.min(jnp.where(g == mx, rowf, float(LANES)), axis=0, keepdims=True)
        hit = (rowf == idx) & (mx > NEG_INF)
        offsets = jnp.where(hit, 0.0, offsets)
        g = jnp.where(hit, NEG_INF, g)
    return offsets


def _moba_prompt_kernel(q_ref, k_ref, vt_ref, o_ref, ka_ref, kb_ref, va_ref, vb_ref, km_ref, s_ref, *, n_blocks):
    i = pl.program_id(2)
    lane = lax.broadcasted_iota(jnp.int32, (MB_BLOCK, LANES), 1)
    first = lane < MB_DH
    feat = lax.broadcasted_iota(jnp.int32, (LANES, MB_BLOCK), 0)
    top = feat < MB_DH

    @pl.when(i == 0)
    def _():
        def prep(j, carry):
            rs = pl.ds(pl.multiple_of(j * MB_BLOCK, MB_BLOCK), MB_BLOCK)
            k = k_ref[rs, :]
            vt = vt_ref[0, :, rs]
            ka_ref[rs, :] = jnp.where(first, k, jnp.where(lane == MB_DH + j, 1.0, 0.0)).astype(BF16)
            kb_ref[rs, :] = jnp.where(first, jnp.where(lane == j, 1.0, 0.0), k).astype(BF16)
            va_ref[:, rs] = jnp.where(top, vt, 1.0).astype(BF16)
            vb_ref[:, rs] = jnp.where(top, 1.0, vt).astype(BF16)
            km_ref[pl.ds(j, 1), :] = jnp.mean(k, axis=0, keepdims=True)
            return carry

        lax.fori_loop(0, n_blocks, prep, 0)

    q = q_ref[...] * (MB_DH ** -0.5)
    heads = []
    for is_a in (True, False):
        qh = jnp.where(first, q, 0.0) if is_a else jnp.where(first, 0.0, q)
        k_ref_h, v_ref_h = (ka_ref, va_ref) if is_a else (kb_ref, vb_ref)
        gate_t = _dot_nt(km_ref[...], qh, precision=lax.Precision.HIGHEST)
        offs = _picked_offsets_by_row(gate_t, i)
        before = MB_DH if is_a else 0
        slots = ([jnp.zeros((before, MB_BLOCK), F32)] if before else []) + [offs]
        slots.append(jnp.zeros((LANES - before - n_blocks, MB_BLOCK), F32))
        sel = jnp.concatenate(slots, axis=0).T
        q_sel = (jnp.where(first, q, sel) if is_a else jnp.where(first, sel, q)).astype(BF16)
        heads.append((qh.astype(BF16), q_sel, k_ref_h, v_ref_h))

    own = pl.ds(pl.multiple_of(i * MB_BLOCK, MB_BLOCK), MB_BLOCK)
    key = lax.broadcasted_iota(jnp.int32, (MB_BLOCK, MB_BLOCK), 0)
    qry = lax.broadcasted_iota(jnp.int32, (MB_BLOCK, MB_BLOCK), 1)
    init = []
    for q_own, _, k_ref_h, v_ref_h in heads:
        s = jnp.where(key <= qry, _dot_nt(k_ref_h[own, :], q_own), NEG_INF)
        m = jnp.max(s, axis=0, keepdims=True)
        init += [m, jnp.dot(v_ref_h[:, own], jnp.exp(s - m).astype(BF16), preferred_element_type=F32)]

    group_rows = MB_GROUP * MB_BLOCK
    last_group = n_blocks // MB_GROUP - 1

    def group(g):
        return pl.ds(pl.multiple_of(g * group_rows, group_rows), group_rows)

    def produce(g, slot):
        rs = group(jnp.minimum(g, last_group))
        cmax = []
        for h, (_, q_sel, k_ref_h, _) in enumerate(heads):
            s = _dot_nt(k_ref_h[rs, :], q_sel)
            s_ref[slot, h] = s
            cmax.append(jnp.max(s, axis=0, keepdims=True))
        return cmax

    def consume(g, slot, cmax, state):
        rs = group(g)
        out = []
        for h, (_, _, _, v_ref_h) in enumerate(heads):
            m, acc = state[2 * h:2 * h + 2]
            m_new = jnp.maximum(m, cmax[h])
            p = jnp.exp(s_ref[slot, h] - m_new).astype(BF16)
            out += [m_new, jnp.exp(m - m_new) * acc + jnp.dot(v_ref_h[:, rs], p, preferred_element_type=F32)]
        return out

    def two_groups(t, carry):
        cmax0, state = list(carry[:2]), list(carry[2:])
        cmax1 = produce(2 * t + 1, 1)
        state = consume(2 * t, 0, cmax0, state)
        cmax0 = produce(2 * t + 2, 0)
        state = consume(2 * t + 1, 1, cmax1, state)
        return tuple(cmax0 + state)

    res = lax.fori_loop(0, (i + 2 * MB_GROUP - 1) // (2 * MB_GROUP), two_groups, tuple(produce(0, 0) + init))
    acc_a, acc_b = res[3], res[5]
    num = jnp.where(top, acc_a, acc_b)
    den = jnp.where(top, acc_a[MB_DH:MB_DH + 1, :], acc_b[:1, :])
    o_ref[...] = (num / den).T.astype(o_ref.dtype)


def _moba_prompt(main, vt, batch, seq):
    n_blocks = seq // MB_BLOCK
    assert n_blocks % 8 == 0 and n_blocks % (2 * MB_GROUP) == 0 and n_blocks <= MB_DH
    pairs = MB_WIDTH // LANES
    q0 = 4 * HG_WIDTH // LANES
    return pl.pallas_call(
        functools.partial(_moba_prompt_kernel, n_blocks=n_blocks),
        grid=(batch, pairs, n_blocks),
        in_specs=[pl.BlockSpec((MB_BLOCK, LANES), lambda b, p, i: (b * n_blocks + i, q0 + p)),
                  pl.BlockSpec((seq, LANES), lambda b, p, i: (b, q0 + pairs + p)),
                  pl.BlockSpec((1, LANES, seq), lambda b, p, i: (b, p, 0))],
        out_specs=pl.BlockSpec((MB_BLOCK, LANES), lambda b, p, i: (b * n_blocks + i, p)),
        out_shape=jax.ShapeDtypeStruct((batch * seq, MB_WIDTH), BF16),
        scratch_shapes=[pltpu.VMEM((seq, LANES), BF16)] * 2 + [pltpu.VMEM((LANES, seq), BF16)] * 2
        + [pltpu.VMEM((n_blocks, LANES), F32), pltpu.VMEM((2, 2, MB_GROUP * MB_BLOCK, MB_BLOCK), F32)],
        compiler_params=_params(("parallel", "parallel", "arbitrary")),
        name="moba_prompt",
    )(main, main, vt)


def _moba_decode_kernel(pt_ref, q_ref, k_ref, v_ref, *refs, n_pages, seq):
    del pt_ref
    k_pages, v_pages, o_ref = refs[:n_pages], refs[n_pages:2 * n_pages], refs[2 * n_pages]
    pages_per_block = MB_BLOCK // PAGE_SIZE
    n_blocks = n_pages // pages_per_block
    rows = seq * MB_HEADS
    sub = lax.broadcasted_iota(jnp.int32, (MB_HEADS, MB_WIDTH), 0)
    head_of_lane = lax.broadcasted_iota(jnp.int32, (MB_HEADS, MB_WIDTH), 1) // MB_DH
    head_mask = sub == head_of_lane
    big = lax.broadcasted_iota(jnp.int32, (LANES, MB_WIDTH), 0)

    def rows_to_tile(x):
        out = jnp.zeros((LANES, MB_WIDTH), F32)
        for t in range(x.shape[0]):
            out = jnp.where(big == t, jnp.broadcast_to(x[t:t + 1, :], out.shape), out)
        return out

    q = q_ref[0] * (MB_DH ** -0.5)
    qbd = jnp.concatenate(
        [jnp.where(head_mask, jnp.broadcast_to(q[t:t + 1, :], head_mask.shape), 0.0) for t in range(seq)], axis=0)
    qbd16 = qbd.astype(BF16)

    lane = lax.broadcasted_iota(jnp.int32, (MB_WIDTH, LANES), 1)
    kmean_t = jnp.zeros((MB_WIDTH, LANES), F32)
    for j in range(n_blocks):
        tot = sum(k_pages[j * pages_per_block + u][0] for u in range(pages_per_block))
        kmean_t = jnp.where(lane == j, jnp.sum(tot, axis=1, keepdims=True) * (1.0 / MB_BLOCK), kmean_t)
    gate = jnp.dot(qbd, kmean_t, preferred_element_type=F32, precision=lax.Precision.HIGHEST)
    glane = lax.broadcasted_iota(jnp.int32, (rows, LANES), 1)
    picks = _top_blocks(gate, glane < n_blocks)

    def block_bias(j):
        hit = (picks[0] == j) | (picks[1] == j) | (picks[2] == j)
        return jnp.where(hit, 0.0, NEG_INF)

    scores = [jnp.dot(qbd16, k_pages[p][0].astype(BF16), preferred_element_type=F32)
              + block_bias(float(p // pages_per_block)) for p in range(n_pages)]
    r = lax.broadcasted_iota(jnp.int32, (rows, LANES), 0) // MB_HEADS
    scores.append(jnp.where(glane <= r, _dot_nt(qbd16, rows_to_tile(k_ref[0]).astype(BF16)), NEG_INF))

    m = functools.reduce(jnp.maximum, scores)
    m = jnp.max(m, axis=-1, keepdims=True)
    l = jnp.zeros((rows, 1), F32)
    acc = jnp.zeros((rows, MB_WIDTH), F32)
    for p, s in enumerate(scores):
        w = jnp.exp(s - m)
        l = l + jnp.sum(w, axis=-1, keepdims=True)
        if p < n_pages:
            acc = acc + _dot_nt(w.astype(BF16), v_pages[p][0].astype(BF16))
        else:
            acc = acc + jnp.dot(w.astype(BF16), rows_to_tile(v_ref[0]).astype(BF16), preferred_element_type=F32)
    acc = acc / l
    for t in range(seq):
        own = jnp.where(head_mask, acc[t * MB_HEADS:(t + 1) * MB_HEADS, :], 0.0)
        o_ref[0, t:t + 1, :] = jnp.sum(own, axis=0, keepdims=True)


def _moba_decode(proj_dec, cache_kt, cache_vt, page_table, layer, n_pool):
    batch, seq, _ = proj_dec.shape
    n_pages = page_table.shape[1]
    q0 = 4 * HG_WIDTH // MB_WIDTH

    def new(c):
        return pl.BlockSpec((1, seq, MB_WIDTH), lambda b, pt: (b, 0, q0 + c))

    def page(j):
        return pl.BlockSpec((1, MB_WIDTH, PAGE_SIZE), lambda b, pt: (layer * n_pool + pt[b * n_pages + j], 0, 0))

    pages = [page(j) for j in range(n_pages)]
    grid_spec = pltpu.PrefetchScalarGridSpec(
        num_scalar_prefetch=1,
        grid=(batch,),
        in_specs=[new(0), new(1), new(2)] + pages + pages,
        out_specs=pl.BlockSpec((1, seq, MB_WIDTH), lambda b, pt: (b, 0, 0)),
    )
    return pl.pallas_call(
        functools.partial(_moba_decode_kernel, n_pages=n_pages, seq=seq),
        grid_spec=grid_spec,
        out_shape=jax.ShapeDtypeStruct((batch, seq, MB_WIDTH), F32),
        compiler_params=_params(("parallel",)),
        name="moba_decode",
    )(page_table.reshape(-1), proj_dec, proj_dec, proj_dec, *([cache_kt] * n_pages), *([cache_vt] * n_pages))


def _lower_bounds(gamma):
    p = jax.nn.softmax(gamma.astype(F32), axis=0)
    return jnp.cumsum(p, axis=0) - p[:1]


def _tokens_last_to_heads(x_t, batch, seq):
    return jnp.transpose(x_t.reshape(batch, MB_HEADS, MB_DH, seq), (0, 3, 1, 2))


def kernel(x_prompt, x_sample, cache_k, cache_v, state_hgrn, page_table, hg_lower_bounds, w_in, w_out, hg_gnorm,
           norm_ffn1, ffn1_wi, ffn1_wo, norm_mix, norm_ffn2, ffn2_wi, ffn2_wo, final_norm):
    batch, seq, _ = x_prompt.shape
    dec_batch, dec_seq, _ = x_sample.shape
    depth, n_pool = cache_k.shape[:2]
    lower = _lower_bounds(hg_lower_bounds)
    xp = x_prompt.reshape(batch * seq, D_MODEL)
    xs = x_sample.reshape(dec_batch * dec_seq, D_MODEL)
    fn = final_norm.reshape(1, D_MODEL)
    cache_kt = jnp.transpose(cache_k, (0, 1, 3, 4, 2)).reshape(depth * n_pool, MB_WIDTH, PAGE_SIZE)
    cache_vt = jnp.transpose(cache_v, (0, 1, 3, 4, 2)).reshape(depth * n_pool, MB_WIDTH, PAGE_SIZE)
    kp, vp, sp, ks, vs, ss = [], [], [], [], [], []
    k0, v0 = D_IN - 2 * MB_WIDTH, D_IN - MB_WIDTH
    for l in range(depth):
        lb = lower[l].reshape(1, HG_WIDTH)
        gn = hg_gnorm[l].reshape(1, HG_D)
        n1, nm, n2 = (a[l].reshape(1, D_MODEL) for a in (norm_ffn1, norm_mix, norm_ffn2))
        wi1, wo1, wi2, wo2 = (a[l].astype(BF16) for a in (ffn1_wi, ffn1_wo, ffn2_wi, ffn2_wo))
        w_in_l, w_out_l = w_in[l].astype(BF16), w_out[l].astype(BF16)
        xp, main, kt, vt = _ffn_inproj_prompt(xp, n1, wi1, wo1, nm, w_in_l[:, :D_MAIN], w_in_l[:, v0:].T, batch, seq)
        o_hg, s_p = _hgrn_prompt(main, lb, gn, batch, seq)
        o_mb = _moba_prompt(main, vt, batch, seq)
        xp = _outproj_ffn(xp, o_hg, o_mb, w_out_l, n2, wi2, wo2, fn, final=(l == depth - 1))
        kp.append(_tokens_last_to_heads(kt, batch, seq))
        vp.append(_tokens_last_to_heads(vt, batch, seq))
        sp.append(s_p)

        xs, proj = _ffn_inproj(xs, n1, wi1, wo1, nm, w_in_l)
        proj_dec = proj.reshape(dec_batch, dec_seq, D_IN)
        o_hg, s_s = _hgrn_decode(proj, lb, gn, state_hgrn, l, dec_batch, dec_seq)
        o_mb = _moba_decode(proj_dec, cache_kt, cache_vt, page_table, l, n_pool)
        o_mb = o_mb.reshape(dec_batch * dec_seq, MB_WIDTH).astype(BF16)
        xs = _outproj_ffn(xs, o_hg, o_mb, w_out_l, n2, wi2, wo2, fn, final=(l == depth - 1))
        ks.append(proj_dec[:, :, k0:v0].reshape(dec_batch, dec_seq, MB_HEADS, MB_DH))
        vs.append(proj_dec[:, :, v0:].reshape(dec_batch, dec_seq, MB_HEADS, MB_DH))
        ss.append(s_s)
    return (xp.reshape(batch, seq, D_MODEL), xs.reshape(dec_batch, dec_seq, D_MODEL), jnp.stack(kp), jnp.stack(vp),
            jnp.stack(sp), jnp.stack(ks), jnp.stack(vs), jnp.stack(ss))
```

```python
import functools

import jax
import jax.numpy as jnp
from jax import lax
from jax.experimental import pallas as pl
from jax.experimental.pallas import tpu as pltpu

F32 = jnp.float32
BF16 = jnp.bfloat16

D_MODEL = 1024
HG_HEADS = 4
HG_D = 128
HG_WIDTH = HG_HEADS * HG_D
MB_HEADS = 8
MB_DH = 64
MB_WIDTH = MB_HEADS * MB_DH
MB_BLOCK = 256
MB_TOPK = 3
D_FF = 2816
D_IN = 4 * HG_WIDTH + 3 * MB_WIDTH
D_MAIN = 4 * HG_WIDTH + 2 * MB_WIDTH
PAGE_SIZE = 128
EPS = 1e-6
FFN_SCALE = 0.5

LANES = 128
FF_CHUNK = 1408
DENSE_ROWS = 256
HG_CHUNK = 128
HG_ROWS = 512
HG_DEC_SEQS = 8
MB_GROUP = 2
ONES_ROWS = 16
MASKED = -(2.0 ** 30)
VMEM_LIMIT = 56 * 1024 * 1024

NEG_INF = float("-inf")


def _resident(shape):
    return pl.BlockSpec(shape, lambda *_: (0,) * len(shape), pipeline_mode=pl.Buffered(1))


def _params(semantics):
    return pltpu.CompilerParams(dimension_semantics=semantics, vmem_limit_bytes=VMEM_LIMIT)


def _rmsnorm(x, g):
    return x * lax.rsqrt(jnp.mean(x * x, axis=-1, keepdims=True) + EPS) * g


def _silu(x):
    return x * jax.nn.sigmoid(x)


def _dot_nt(a, b, **kw):
    return lax.dot_general(a, b, (((1,), (1,)), ((), ())), preferred_element_type=F32, **kw)


def _dot_tn(a, b, **kw):
    return lax.dot_general(a, b, (((0,), (0,)), ((), ())), preferred_element_type=F32, **kw)


def _swiglu_half_step(x, n_ref, wi_ref, wo_ref):
    h = _rmsnorm(x, n_ref[...]).astype(BF16)
    acc = jnp.zeros_like(x)
    for c in range(D_FF // FF_CHUNK):
        lo = c * FF_CHUNK
        a = jnp.dot(h, wi_ref[:, lo:lo + FF_CHUNK], preferred_element_type=F32)
        b = jnp.dot(h, wi_ref[:, D_FF + lo:D_FF + lo + FF_CHUNK], preferred_element_type=F32)
        g = (_silu(a) * b).astype(BF16)
        acc = acc + jnp.dot(g, wo_ref[lo:lo + FF_CHUNK, :], preferred_element_type=F32)
    return x + FFN_SCALE * acc


def _ffn_inproj_kernel(x_ref, n1_ref, wi_ref, wo_ref, nm_ref, win_ref, x1_ref, proj_ref):
    x1 = _swiglu_half_step(x_ref[...], n1_ref, wi_ref, wo_ref)
    x1_ref[...] = x1
    h = _rmsnorm(x1, nm_ref[...]).astype(BF16)
    proj_ref[...] = jnp.dot(h, win_ref[...], preferred_element_type=F32)


def _ffn_inproj(x, n1, wi, wo, nm, w_in):
    n = x.shape[0]
    rows = pl.BlockSpec((DENSE_ROWS, D_MODEL), lambda i: (i, 0))
    return pl.pallas_call(
        _ffn_inproj_kernel,
        grid=(n // DENSE_ROWS,),
        in_specs=[rows, _resident((1, D_MODEL)), _resident((D_MODEL, 2 * D_FF)), _resident((D_FF, D_MODEL)),
                  _resident((1, D_MODEL)), _resident((D_MODEL, D_IN))],
        out_specs=[rows, pl.BlockSpec((DENSE_ROWS, D_IN), lambda i: (i, 0))],
        out_shape=[jax.ShapeDtypeStruct((n, D_MODEL), F32), jax.ShapeDtypeStruct((n, D_IN), F32)],
        compiler_params=_params(("parallel",)),
        name="ffn_inproj",
    )(x, n1, wi, wo, nm, w_in)


def _ffn_inproj_prompt_kernel(x_ref, n1_ref, wi_ref, wo_ref, nm_ref, wmain_ref, wvt_ref, x1_ref, main_ref, kt_ref, vt_ref):
    x1 = _swiglu_half_step(x_ref[...], n1_ref, wi_ref, wo_ref)
    x1_ref[...] = x1
    h = _rmsnorm(x1, nm_ref[...]).astype(BF16)
    main = jnp.dot(h, wmain_ref[...], preferred_element_type=F32)
    main_ref[...] = main
    kt_ref[0] = main[:, D_MAIN - MB_WIDTH:].T
    vt_ref[0] = _dot_nt(wvt_ref[...], h)


def _ffn_inproj_prompt(x, n1, wi, wo, nm, w_main, w_vt, batch, seq):
    steps = seq // DENSE_ROWS
    rows = pl.BlockSpec((DENSE_ROWS, D_MODEL), lambda b, i: (b * steps + i, 0))
    tr = pl.BlockSpec((1, MB_WIDTH, DENSE_ROWS), lambda b, i: (b, 0, i))
    return pl.pallas_call(
        _ffn_inproj_prompt_kernel,
        grid=(batch, steps),
        in_specs=[rows, _resident((1, D_MODEL)), _resident((D_MODEL, 2 * D_FF)), _resident((D_FF, D_MODEL)),
                  _resident((1, D_MODEL)), _resident((D_MODEL, D_MAIN)), _resident((MB_WIDTH, D_MODEL))],
        out_specs=[rows, pl.BlockSpec((DENSE_ROWS, D_MAIN), lambda b, i: (b * steps + i, 0)), tr, tr],
        out_shape=[jax.ShapeDtypeStruct((batch * seq, D_MODEL), F32), jax.ShapeDtypeStruct((batch * seq, D_MAIN), F32),
                   jax.ShapeDtypeStruct((batch, MB_WIDTH, seq), F32), jax.ShapeDtypeStruct((batch, MB_WIDTH, seq), F32)],
        compiler_params=_params(("parallel", "parallel")),
        name="ffn_inproj_prompt",
    )(x, n1, wi, wo, nm, w_main, w_vt)


def _outproj_ffn_kernel(x_ref, ohg_ref, omb_ref, wout_ref, n2_ref, wi_ref, wo_ref, fn_ref, y_ref, *, final):
    x = x_ref[...]
    x = x + jnp.dot(ohg_ref[...], wout_ref[:HG_WIDTH, :], preferred_element_type=F32)
    x = x + jnp.dot(omb_ref[...], wout_ref[HG_WIDTH:, :], preferred_element_type=F32)
    x = _swiglu_half_step(x, n2_ref, wi_ref, wo_ref)
    if final:
        x = _rmsnorm(x, fn_ref[...])
    y_ref[...] = x


def _outproj_ffn(x, o_hg, o_mb, w_out, n2, wi, wo, fn, final):
    n = x.shape[0]
    rows = pl.BlockSpec((DENSE_ROWS, D_MODEL), lambda i: (i, 0))
    half = pl.BlockSpec((DENSE_ROWS, HG_WIDTH), lambda i: (i, 0))
    return pl.pallas_call(
        functools.partial(_outproj_ffn_kernel, final=final),
        grid=(n // DENSE_ROWS,),
        in_specs=[rows, half, half, _resident((D_MODEL, D_MODEL)), _resident((1, D_MODEL)),
                  _resident((D_MODEL, 2 * D_FF)), _resident((D_FF, D_MODEL)), _resident((1, D_MODEL))],
        out_specs=rows,
        out_shape=jax.ShapeDtypeStruct((n, D_MODEL), F32),
        compiler_params=_params(("parallel",)),
        name="outproj_ffn",
    )(x, o_hg, o_mb, w_out, n2, wi, wo, fn)


def _segment_products(f, top):
    rows = f.shape[0]
    t = lax.broadcasted_iota(jnp.int32, f.shape, 0)
    gq, gk, tot = f, jnp.ones_like(f), f
    out = {1: (gq, gk)}
    m = 1
    while m < top:
        second = (t & m) != 0
        up = pltpu.roll(tot, m, 0)
        dn = pltpu.roll(tot, rows - m, 0)
        gq = jnp.where(second, gq * up, gq)
        gk = jnp.where(second, gk, gk * dn)
        tot = tot * jnp.where(second, up, dn)
        m *= 2
        out[m] = (gq, gk)
    return out


def _intra_scores(q, k, prods, top):
    rows = q.shape[0]
    t = lax.broadcasted_iota(jnp.int32, (rows, rows), 0)
    s = lax.broadcasted_iota(jnp.int32, (rows, rows), 1)
    ts = t ^ s
    a = jnp.where(t == s, _dot_nt(q.astype(BF16), k.astype(BF16)), 0.0)
    m = 1
    while m < top:
        gq, gk = prods[m]
        p = _dot_nt((q * gq).astype(BF16), (k * gk).astype(BF16))
        a = jnp.where((ts >= m) & (ts < 2 * m) & (t > s), p, a)
        m *= 2
    return a


def _hgrn_inputs(hq, z, lb):
    sig = jax.nn.sigmoid(z)
    f = lb + (1.0 - lb) * sig
    k = (1.0 - lb) * jax.nn.sigmoid(-z)
    return _silu(hq), k, f


def _hgrn_finish(o, hg, gn):
    return _rmsnorm(o, gn) * _silu(hg)


def _hgrn_prompt_kernel(hq_ref, hf_ref, hi_ref, hg_ref, lb_ref, gn_ref, o_ref, s_ref, st_ref):
    step = pl.program_id(1)

    @pl.when(step == 0)
    def _():
        st_ref[...] = jnp.zeros_like(st_ref)

    def chunk_body(j, carry):
        r0 = pl.multiple_of(j * HG_CHUNK, HG_CHUNK)
        rs = pl.ds(r0, HG_CHUNK)
        for h in range(HG_HEADS):
            cs = slice(h * HG_D, (h + 1) * HG_D)
            q, k, f = _hgrn_inputs(hq_ref[rs, cs], hf_ref[rs, cs], lb_ref[:, cs])
            v = hi_ref[rs, cs].astype(BF16)
            prods = _segment_products(f, HG_CHUNK)
            a = _intra_scores(q, k, prods, HG_CHUNK)
            gq, gk = prods[HG_CHUNK]
            st = st_ref[h]
            o = jnp.dot(a.astype(BF16), v, preferred_element_type=F32)
            o = o + _dot_nt((q * gq).astype(BF16), st.astype(BF16))
            st_ref[h] = st * gq[HG_CHUNK - 1:HG_CHUNK, :] + _dot_tn(v, (k * gk).astype(BF16))
            o_ref[rs, cs] = _hgrn_finish(o, hg_ref[rs, cs], gn_ref[...]).astype(o_ref.dtype)
        return carry

    lax.fori_loop(0, HG_ROWS // HG_CHUNK, chunk_body, 0)

    @pl.when(step == pl.num_programs(1) - 1)
    def _():
        for h in range(HG_HEADS):
            s_ref[0, h] = st_ref[h].T


def _hgrn_prompt(proj, lb, gn, batch, seq):
    steps = seq // HG_ROWS

    def col(c):
        return pl.BlockSpec((HG_ROWS, HG_WIDTH), lambda b, i: (b * steps + i, c))

    return pl.pallas_call(
        _hgrn_prompt_kernel,
        grid=(batch, steps),
        in_specs=[col(0), col(1), col(2), col(3), _resident((1, HG_WIDTH)), _resident((1, HG_D))],
        out_specs=[pl.BlockSpec((HG_ROWS, HG_WIDTH), lambda b, i: (b * steps + i, 0)),
                   pl.BlockSpec((1, HG_HEADS, HG_D, HG_D), lambda b, i: (b, 0, 0, 0))],
        out_shape=[jax.ShapeDtypeStruct((batch * seq, HG_WIDTH), BF16),
                   jax.ShapeDtypeStruct((batch, HG_HEADS, HG_D, HG_D), F32)],
        scratch_shapes=[pltpu.VMEM((HG_HEADS, HG_D, HG_D), F32)],
        compiler_params=_params(("parallel", "arbitrary")),
        name="hgrn_prompt",
    )(proj, proj, proj, proj, lb, gn)


def _hgrn_decode_kernel(hq_ref, hf_ref, hi_ref, hg_ref, lb_ref, gn_ref, s0_ref, o_ref, s_ref, *, seq):
    rows = HG_DEC_SEQS * seq
    row = lax.broadcasted_iota(jnp.int32, (rows, HG_D), 0)
    for h in range(HG_HEADS):
        cs = slice(h * HG_D, (h + 1) * HG_D)
        q, k, f = _hgrn_inputs(hq_ref[:, cs], hf_ref[:, cs], lb_ref[:, cs])
        v = hi_ref[:, cs].astype(BF16)
        prods = _segment_products(f, seq)
        a = _intra_scores(q, k, prods, seq)
        gq, gk = prods[seq]
        qg, kg = q * gq, k * gk
        o = jnp.dot(a.astype(BF16), v, preferred_element_type=F32)
        for i in range(HG_DEC_SEQS):
            mine = (row >= i * seq) & (row < (i + 1) * seq)
            s0 = s0_ref[0, i, h]
            o = o + jnp.dot(jnp.where(mine, qg, 0.0).astype(BF16), s0.astype(BF16), preferred_element_type=F32)
            last = (i + 1) * seq - 1
            decay = jnp.broadcast_to(gq[last:last + 1, :], (HG_D, HG_D)).T
            s_ref[i, h] = s0 * decay + _dot_tn(jnp.where(mine, kg, 0.0).astype(BF16), v)
        o_ref[:, cs] = _hgrn_finish(o, hg_ref[:, cs], gn_ref[...]).astype(o_ref.dtype)


def _hgrn_decode(proj, lb, gn, state, layer, batch, seq):
    rows = HG_DEC_SEQS * seq

    def col(c):
        return pl.BlockSpec((rows, HG_WIDTH), lambda g: (g, c))

    return pl.pallas_call(
        functools.partial(_hgrn_decode_kernel, seq=seq),
        grid=(batch // HG_DEC_SEQS,),
        in_specs=[col(0), col(1), col(2), col(3), _resident((1, HG_WIDTH)), _resident((1, HG_D)),
                  pl.BlockSpec((1, HG_DEC_SEQS, HG_HEADS, HG_D, HG_D), lambda g: (layer, g, 0, 0, 0))],
        out_specs=[pl.BlockSpec((rows, HG_WIDTH), lambda g: (g, 0)),
                   pl.BlockSpec((HG_DEC_SEQS, HG_HEADS, HG_D, HG_D), lambda g: (g, 0, 0, 0))],
        out_shape=[jax.ShapeDtypeStruct((batch * seq, HG_WIDTH), BF16),
                   jax.ShapeDtypeStruct((batch, HG_HEADS, HG_D, HG_D), F32)],
        compiler_params=_params(("parallel",)),
        name="hgrn_decode",
    )(proj, proj, proj, proj, lb, gn, state)


def _top_blocks(gate, valid):
    lanef = lax.broadcasted_iota(jnp.int32, gate.shape, 1).astype(F32)
    g = jnp.where(valid, gate, NEG_INF)
    picks = []
    for _ in range(MB_TOPK):
        mx = jnp.max(g, axis=-1, keepdims=True)
        idx = jnp.min(jnp.where(g == mx, lanef, float(LANES)), axis=-1, keepdims=True)
        picks.append(jnp.where(mx > NEG_INF, idx, -1.0))
        g = jnp.where(lanef == idx, NEG_INF, g)
    return picks


def _picked_offsets_by_row(gate_t, n_valid):
    rowi = lax.broadcasted_iota(jnp.int32, gate_t.shape, 0)
    rowf = rowi.astype(F32)
    g = jnp.where(rowi < n_valid, gate_t, NEG_INF)
    offsets = jnp.full(gate_t.shape, MASKED, F32)
    for _ in range(MB_TOPK):
        mx = jnp.max(g, axis=0, keepdims=True)
        idx = jnp.min(jnp.where(g == mx, rowf, float(LANES)), axis=0, keepdims=True)
        hit = (rowf == idx) & (mx > NEG_INF)
        offsets = jnp.where(hit, 0.0, offsets)
        g = jnp.where(hit, NEG_INF, g)
    return offsets


def _moba_prompt_kernel(q_ref, qn_ref, k_ref, vt_ref, o_ref, ka_ref, kb_ref, va_ref, vb_ref, km_ref, s_ref, qsel_ref, *,
                        n_blocks):
    i = pl.program_id(2)
    lane = lax.broadcasted_iota(jnp.int32, (MB_BLOCK, LANES), 1)
    first = lane < MB_DH
    feat = lax.broadcasted_iota(jnp.int32, (LANES, MB_BLOCK), 0)
    top = feat < MB_DH

    @pl.when(i == 0)
    def _():
        qsel_ref[0] = jnp.zeros(qsel_ref.shape[1:], BF16)

        def prep(j, carry):
            rs = pl.ds(pl.multiple_of(j * MB_BLOCK, MB_BLOCK), MB_BLOCK)
            k = k_ref[rs, :]
            vt = vt_ref[0, :, rs]
            ka_ref[rs, :] = jnp.where(first, k, jnp.where(lane == MB_DH + j, 1.0, 0.0)).astype(BF16)
            kb_ref[rs, :] = jnp.where(first, jnp.where(lane == j, 1.0, 0.0), k).astype(BF16)
            va_ref[:, rs] = jnp.where(top, vt, 1.0).astype(BF16)
            vb_ref[:, rs] = jnp.where(top, 1.0, vt).astype(BF16)
            km_ref[pl.ds(j, 1), :] = jnp.mean(k, axis=0, keepdims=True)
            return carry

        lax.fori_loop(0, n_blocks, prep, 0)

    q = q_ref[...] * (MB_DH ** -0.5)
    heads = [(jnp.where(first, q, 0.0).astype(BF16), qsel_ref[i % 2, 0], ka_ref, va_ref),
             (jnp.where(first, 0.0, q).astype(BF16), qsel_ref[i % 2, 1], kb_ref, vb_ref)]
    v_rows = (slice(0, MB_DH + ONES_ROWS), slice(MB_DH - ONES_ROWS, LANES))

    def select_for_next_tile():
        q_next = qn_ref[...] * (MB_DH ** -0.5)
        for h, is_a in enumerate((True, False)):
            qh = jnp.where(first, q_next, 0.0) if is_a else jnp.where(first, 0.0, q_next)
            gate_t = _dot_nt(km_ref[...], qh, precision=lax.Precision.HIGHEST)
            offs = _picked_offsets_by_row(gate_t, i + 1)
            before = MB_DH if is_a else 0
            slots = ([jnp.zeros((before, MB_BLOCK), F32)] if before else []) + [offs]
            slots.append(jnp.zeros((LANES - before - n_blocks, MB_BLOCK), F32))
            sel = jnp.concatenate(slots, axis=0).T
            q_sel = jnp.where(first, q_next, sel) if is_a else jnp.where(first, sel, q_next)
            qsel_ref[(i + 1) % 2, h] = q_sel.astype(BF16)

    group_rows = MB_GROUP * MB_BLOCK
    last_group = n_blocks // MB_GROUP - 1

    def group(g):
        return pl.ds(pl.multiple_of(g * group_rows, group_rows), group_rows)

    def produce(g, slot):
        rs = group(jnp.minimum(g, last_group))
        cmax = []
        for h, (_, q_sel, k_ref_h, _) in enumerate(heads):
            s = _dot_nt(k_ref_h[rs, :], q_sel)
            s_ref[slot, h] = s
            cmax.append(jnp.max(s, axis=0, keepdims=True))
        return cmax

    def consume(g, slot, cmax, state):
        rs = group(g)
        out = []
        for h, (_, _, _, v_ref_h) in enumerate(heads):
            m, acc = state[2 * h:2 * h + 2]
            m_new = jnp.maximum(m, cmax[h])
            p = jnp.exp(s_ref[slot, h] - m_new).astype(BF16)
            out += [m_new, jnp.exp(m - m_new) * acc + jnp.dot(v_ref_h[v_rows[h], rs], p, preferred_element_type=F32)]
        return out

    def two_groups(t, carry):
        cmax0, state = list(carry[:2]), list(carry[2:])
        cmax1 = produce(2 * t + 1, 1)
        state = consume(2 * t, 0, cmax0, state)
        cmax0 = produce(2 * t + 2, 0)
        state = consume(2 * t + 1, 1, cmax1, state)
        return tuple(cmax0 + state)

    select_for_next_tile()
    own = pl.ds(pl.multiple_of(i * MB_BLOCK, MB_BLOCK), MB_BLOCK)
    key = lax.broadcasted_iota(jnp.int32, (MB_BLOCK, MB_BLOCK), 0)
    qry = lax.broadcasted_iota(jnp.int32, (MB_BLOCK, MB_BLOCK), 1)
    own_scores = [jnp.where(key <= qry, _dot_nt(k_ref_h[own, :], q_own), NEG_INF) for q_own, _, k_ref_h, _ in heads]
    cmax0 = produce(0, 0)
    init = []
    for h, (s, (_, _, _, v_ref_h)) in enumerate(zip(own_scores, heads)):
        m = jnp.max(s, axis=0, keepdims=True)
        init += [m, jnp.dot(v_ref_h[v_rows[h], own], jnp.exp(s - m).astype(BF16), preferred_element_type=F32)]
    res = lax.fori_loop(0, (i + 2 * MB_GROUP - 1) // (2 * MB_GROUP), two_groups, tuple(cmax0 + init))
    acc_a, acc_b = res[3], res[5]
    num = jnp.concatenate([acc_a[:MB_DH], acc_b[ONES_ROWS:]], axis=0)
    den = jnp.where(top, acc_a[MB_DH:MB_DH + 1, :], acc_b[:1, :])
    o_ref[...] = (num / den).T.astype(o_ref.dtype)


def _moba_prompt(main, vt, batch, seq):
    n_blocks = seq // MB_BLOCK
    assert n_blocks % 8 == 0 and n_blocks % (2 * MB_GROUP) == 0 and n_blocks <= MB_DH
    pairs = MB_WIDTH // LANES
    q0 = 4 * HG_WIDTH // LANES
    return pl.pallas_call(
        functools.partial(_moba_prompt_kernel, n_blocks=n_blocks),
        grid=(batch, pairs, n_blocks),
        in_specs=[pl.BlockSpec((MB_BLOCK, LANES), lambda b, p, i: (b * n_blocks + i, q0 + p)),
                  pl.BlockSpec((MB_BLOCK, LANES),
                               lambda b, p, i: (b * n_blocks + jnp.minimum(i + 1, n_blocks - 1), q0 + p)),
                  pl.BlockSpec((seq, LANES), lambda b, p, i: (b, q0 + pairs + p)),
                  pl.BlockSpec((1, LANES, seq), lambda b, p, i: (b, p, 0))],
        out_specs=pl.BlockSpec((MB_BLOCK, LANES), lambda b, p, i: (b * n_blocks + i, p)),
        out_shape=jax.ShapeDtypeStruct((batch * seq, MB_WIDTH), BF16),
        scratch_shapes=[pltpu.VMEM((seq, LANES), BF16)] * 2 + [pltpu.VMEM((LANES, seq), BF16)] * 2
        + [pltpu.VMEM((n_blocks, LANES), F32), pltpu.VMEM((2, 2, MB_GROUP * MB_BLOCK, MB_BLOCK), F32),
           pltpu.VMEM((2, 2, MB_BLOCK, LANES), BF16)],
        compiler_params=_params(("parallel", "parallel", "arbitrary")),
        name="moba_prompt",
    )(main, main, main, vt)


def _moba_decode_kernel(pt_ref, q_ref, k_ref, v_ref, *refs, n_pages, seq):
    del pt_ref
    k_pages, v_pages, o_ref = refs[:n_pages], refs[n_pages:2 * n_pages], refs[2 * n_pages]
    pages_per_block = MB_BLOCK // PAGE_SIZE
    n_blocks = n_pages // pages_per_block
    rows = seq * MB_HEADS
    sub = lax.broadcasted_iota(jnp.int32, (MB_HEADS, MB_WIDTH), 0)
    head_of_lane = lax.broadcasted_iota(jnp.int32, (MB_HEADS, MB_WIDTH), 1) // MB_DH
    head_mask = sub == head_of_lane
    big = lax.broadcasted_iota(jnp.int32, (LANES, MB_WIDTH), 0)

    def rows_to_tile(x):
        out = jnp.zeros((LANES, MB_WIDTH), F32)
        for t in range(x.shape[0]):
            out = jnp.where(big == t, jnp.broadcast_to(x[t:t + 1, :], out.shape), out)
        return out

    q = q_ref[0] * (MB_DH ** -0.5)
    qbd = jnp.concatenate(
        [jnp.where(head_mask, jnp.broadcast_to(q[t:t + 1, :], head_mask.shape), 0.0) for t in range(seq)], axis=0)
    qbd16 = qbd.astype(BF16)

    lane = lax.broadcasted_iota(jnp.int32, (MB_WIDTH, LANES), 1)
    kmean_t = jnp.zeros((MB_WIDTH, LANES), F32)
    for j in range(n_blocks):
        tot = sum(k_pages[j * pages_per_block + u][0] for u in range(pages_per_block))
        kmean_t = jnp.where(lane == j, jnp.sum(tot, axis=1, keepdims=True) * (1.0 / MB_BLOCK), kmean_t)
    gate = jnp.dot(qbd, kmean_t, preferred_element_type=F32, precision=lax.Precision.HIGHEST)
    glane = lax.broadcasted_iota(jnp.int32, (rows, LANES), 1)
    picks = _top_blocks(gate, glane < n_blocks)

    def block_bias(j):
        hit = (picks[0] == j) | (picks[1] == j) | (picks[2] == j)
        return jnp.where(hit, 0.0, NEG_INF)

    scores = [jnp.dot(qbd16, k_pages[p][0].astype(BF16), preferred_element_type=F32)
              + block_bias(float(p // pages_per_block)) for p in range(n_pages)]
    r = lax.broadcasted_iota(jnp.int32, (rows, LANES), 0) // MB_HEADS
    scores.append(jnp.where(glane <= r, _dot_nt(qbd16, rows_to_tile(k_ref[0]).astype(BF16)), NEG_INF))

    m = functools.reduce(jnp.maximum, scores)
    m = jnp.max(m, axis=-1, keepdims=True)
    l = jnp.zeros((rows, 1), F32)
    acc = jnp.zeros((rows, MB_WIDTH), F32)
    for p, s in enumerate(scores):
        w = jnp.exp(s - m)
        l = l + jnp.sum(w, axis=-1, keepdims=True)
        if p < n_pages:
            acc = acc + _dot_nt(w.astype(BF16), v_pages[p][0].astype(BF16))
        else:
            acc = acc + jnp.dot(w.astype(BF16), rows_to_tile(v_ref[0]).astype(BF16), preferred_element_type=F32)
    acc = acc / l
    for t in range(seq):
        own = jnp.where(head_mask, acc[t * MB_HEADS:(t + 1) * MB_HEADS, :], 0.0)
        o_ref[0, t:t + 1, :] = jnp.sum(own, axis=0, keepdims=True)


def _moba_decode(proj_dec, cache_kt, cache_vt, page_table, layer, n_pool):
    batch, seq, _ = proj_dec.shape
    n_pages = page_table.shape[1]
    q0 = 4 * HG_WIDTH // MB_WIDTH

    def new(c):
        return pl.BlockSpec((1, seq, MB_WIDTH), lambda b, pt: (b, 0, q0 + c))

    def page(j):
        return pl.BlockSpec((1, MB_WIDTH, PAGE_SIZE), lambda b, pt: (layer * n_pool + pt[b * n_pages + j], 0, 0))

    pages = [page(j) for j in range(n_pages)]
    grid_spec = pltpu.PrefetchScalarGridSpec(
        num_scalar_prefetch=1,
        grid=(batch,),
        in_specs=[new(0), new(1), new(2)] + pages + pages,
        out_specs=pl.BlockSpec((1, seq, MB_WIDTH), lambda b, pt: (b, 0, 0)),
    )
    return pl.pallas_call(
        functools.partial(_moba_decode_kernel, n_pages=n_pages, seq=seq),
        grid_spec=grid_spec,
        out_shape=jax.ShapeDtypeStruct((batch, seq, MB_WIDTH), F32),
        compiler_params=_params(("parallel",)),
        name="moba_decode",
    )(page_table.reshape(-1), proj_dec, proj_dec, proj_dec, *([cache_kt] * n_pages), *([cache_vt] * n_pages))


def _lower_bounds(gamma):
    p = jax.nn.softmax(gamma.astype(F32), axis=0)
    return jnp.cumsum(p, axis=0) - p[:1]


def _tokens_last_to_heads(x_t, batch, seq):
    return jnp.transpose(x_t.reshape(batch, MB_HEADS, MB_DH, seq), (0, 3, 1, 2))


def kernel(x_prompt, x_sample, cache_k, cache_v, state_hgrn, page_table, hg_lower_bounds, w_in, w_out, hg_gnorm,
           norm_ffn1, ffn1_wi, ffn1_wo, norm_mix, norm_ffn2, ffn2_wi, ffn2_wo, final_norm):
    batch, seq, _ = x_prompt.shape
    dec_batch, dec_seq, _ = x_sample.shape
    depth, n_pool = cache_k.shape[:2]
    lower = _lower_bounds(hg_lower_bounds)
    xp = x_prompt.reshape(batch * seq, D_MODEL)
    xs = x_sample.reshape(dec_batch * dec_seq, D_MODEL)
    fn = final_norm.reshape(1, D_MODEL)
    cache_kt = jnp.transpose(cache_k, (0, 1, 3, 4, 2)).reshape(depth * n_pool, MB_WIDTH, PAGE_SIZE)
    cache_vt = jnp.transpose(cache_v, (0, 1, 3, 4, 2)).reshape(depth * n_pool, MB_WIDTH, PAGE_SIZE)
    kp, vp, sp, ks, vs, ss = [], [], [], [], [], []
    k0, v0 = D_IN - 2 * MB_WIDTH, D_IN - MB_WIDTH
    for l in range(depth):
        lb = lower[l].reshape(1, HG_WIDTH)
        gn = hg_gnorm[l].reshape(1, HG_D)
        n1, nm, n2 = (a[l].reshape(1, D_MODEL) for a in (norm_ffn1, norm_mix, norm_ffn2))
        wi1, wo1, wi2, wo2 = (a[l].astype(BF16) for a in (ffn1_wi, ffn1_wo, ffn2_wi, ffn2_wo))
        w_in_l, w_out_l = w_in[l].astype(BF16), w_out[l].astype(BF16)
        xp, main, kt, vt = _ffn_inproj_prompt(xp, n1, wi1, wo1, nm, w_in_l[:, :D_MAIN], w_in_l[:, v0:].T, batch, seq)
        o_hg, s_p = _hgrn_prompt(main, lb, gn, batch, seq)
        o_mb = _moba_prompt(main, vt, batch, seq)
        xp = _outproj_ffn(xp, o_hg, o_mb, w_out_l, n2, wi2, wo2, fn, final=(l == depth - 1))
        kp.append(_tokens_last_to_heads(kt, batch, seq))
        vp.append(_tokens_last_to_heads(vt, batch, seq))
        sp.append(s_p)

        xs, proj = _ffn_inproj(xs, n1, wi1, wo1, nm, w_in_l)
        proj_dec = proj.reshape(dec_batch, dec_seq, D_IN)
        o_hg, s_s = _hgrn_decode(proj, lb, gn, state_hgrn, l, dec_batch, dec_seq)
        o_mb = _moba_decode(proj_dec, cache_kt, cache_vt, page_table, l, n_pool)
        o_mb = o_mb.reshape(dec_batch * dec_seq, MB_WIDTH).astype(BF16)
        xs = _outproj_ffn(xs, o_hg, o_mb, w_out_l, n2, wi2, wo2, fn, final=(l == depth - 1))
        ks.append(proj_dec[:, :, k0:v0].reshape(dec_batch, dec_seq, MB_HEADS, MB_DH))
        vs.append(proj_dec[:, :, v0:].reshape(dec_batch, dec_seq, MB_HEADS, MB_DH))
        ss.append(s_s)
    return (xp.reshape(batch, seq, D_MODEL), xs.reshape(dec_batch, dec_seq, D_MODEL), jnp.stack(kp), jnp.stack(vp),
            jnp.stack(sp), jnp.stack(ks), jnp.stack(vs), jnp.stack(ss))
```

```python
import functools

import jax
import jax.numpy as jnp
from jax import lax
from jax.experimental import pallas as pl
from jax.experimental.pallas import tpu as pltpu

F32 = jnp.float32
BF16 = jnp.bfloat16

D_MODEL = 1024
HG_HEADS = 4
HG_D = 128
HG_WIDTH = HG_HEADS * HG_D
MB_HEADS = 8
MB_DH = 64
MB_WIDTH = MB_HEADS * MB_DH
MB_BLOCK = 256
MB_TOPK = 3
D_FF = 2816
D_IN = 4 * HG_WIDTH + 3 * MB_WIDTH
D_MAIN = 4 * HG_WIDTH + 2 * MB_WIDTH
PAGE_SIZE = 128
EPS = 1e-6
FFN_SCALE = 0.5

LANES = 128
MXU_WIDTH = 256
FF_SPLITS = (0, 6 * MXU_WIDTH, D_FF)
DENSE_ROWS = 256
HG_CHUNK = 128
HG_ROWS = 512
HG_DEC_SEQS = 8
MB_GROUP = 2
LOG2_E = 1.4426950408889634
SCORE_SCALE = MB_DH ** -0.5 * LOG2_E
ONES_ROWS = 16
MASKED = -(2.0 ** 30)
VMEM_LIMIT = 56 * 1024 * 1024

NEG_INF = float("-inf")


def _resident(shape):
    return pl.BlockSpec(shape, lambda *_: (0,) * len(shape), pipeline_mode=pl.Buffered(1))


def _params(semantics):
    return pltpu.CompilerParams(dimension_semantics=semantics, vmem_limit_bytes=VMEM_LIMIT)


def _rmsnorm(x, g):
    return x * lax.rsqrt(jnp.mean(x * x, axis=-1, keepdims=True) + EPS) * g


def _silu(x):
    return x * jax.nn.sigmoid(x)


def _dot_nt(a, b, **kw):
    return lax.dot_general(a, b, (((1,), (1,)), ((), ())), preferred_element_type=F32, **kw)


def _dot_tn(a, b, **kw):
    return lax.dot_general(a, b, (((0,), (0,)), ((), ())), preferred_element_type=F32, **kw)


def _swiglu_half_step(x, n_ref, wi_ref, wo_ref):
    h = _rmsnorm(x, n_ref[...]).astype(BF16)
    acc = jnp.zeros_like(x)
    for lo, hi in zip(FF_SPLITS[:-1], FF_SPLITS[1:]):
        a = jnp.dot(h, wi_ref[:, lo:hi], preferred_element_type=F32)
        b = jnp.dot(h, wi_ref[:, D_FF + lo:D_FF + hi], preferred_element_type=F32)
        g = (_silu(a) * b).astype(BF16)
        acc = acc + jnp.dot(g, wo_ref[lo:hi, :], preferred_element_type=F32)
    return x + FFN_SCALE * acc


def _ffn_inproj_kernel(x_ref, n1_ref, wi_ref, wo_ref, nm_ref, win_ref, x1_ref, proj_ref):
    x1 = _swiglu_half_step(x_ref[...], n1_ref, wi_ref, wo_ref)
    x1_ref[...] = x1
    h = _rmsnorm(x1, nm_ref[...]).astype(BF16)
    proj_ref[...] = jnp.dot(h, win_ref[...], preferred_element_type=F32)


def _ffn_inproj(x, n1, wi, wo, nm, w_in):
    n = x.shape[0]
    rows = pl.BlockSpec((DENSE_ROWS, D_MODEL), lambda i: (i, 0))
    return pl.pallas_call(
        _ffn_inproj_kernel,
        grid=(n // DENSE_ROWS,),
        in_specs=[rows, _resident((1, D_MODEL)), _resident((D_MODEL, 2 * D_FF)), _resident((D_FF, D_MODEL)),
                  _resident((1, D_MODEL)), _resident((D_MODEL, D_IN))],
        out_specs=[rows, pl.BlockSpec((DENSE_ROWS, D_IN), lambda i: (i, 0))],
        out_shape=[jax.ShapeDtypeStruct((n, D_MODEL), F32), jax.ShapeDtypeStruct((n, D_IN), F32)],
        compiler_params=_params(("parallel",)),
        name="ffn_inproj",
    )(x, n1, wi, wo, nm, w_in)


def _ffn_inproj_prompt_kernel(x_ref, n1_ref, wi_ref, wo_ref, nm_ref, wmain_ref, wvt_ref, x1_ref, main_ref, kt_ref, vt_ref):
    x1 = _swiglu_half_step(x_ref[...], n1_ref, wi_ref, wo_ref)
    x1_ref[...] = x1
    h = _rmsnorm(x1, nm_ref[...]).astype(BF16)
    main = jnp.dot(h, wmain_ref[...], preferred_element_type=F32)
    main_ref[...] = main
    kt_ref[0] = main[:, D_MAIN - MB_WIDTH:].T
    vt_ref[0] = _dot_nt(wvt_ref[...], h)


def _ffn_inproj_prompt(x, n1, wi, wo, nm, w_main, w_vt, batch, seq):
    steps = seq // DENSE_ROWS
    rows = pl.BlockSpec((DENSE_ROWS, D_MODEL), lambda b, i: (b * steps + i, 0))
    tr = pl.BlockSpec((1, MB_WIDTH, DENSE_ROWS), lambda b, i: (b, 0, i))
    return pl.pallas_call(
        _ffn_inproj_prompt_kernel,
        grid=(batch, steps),
        in_specs=[rows, _resident((1, D_MODEL)), _resident((D_MODEL, 2 * D_FF)), _resident((D_FF, D_MODEL)),
                  _resident((1, D_MODEL)), _resident((D_MODEL, D_MAIN)), _resident((MB_WIDTH, D_MODEL))],
        out_specs=[rows, pl.BlockSpec((DENSE_ROWS, D_MAIN), lambda b, i: (b * steps + i, 0)), tr, tr],
        out_shape=[jax.ShapeDtypeStruct((batch * seq, D_MODEL), F32), jax.ShapeDtypeStruct((batch * seq, D_MAIN), F32),
                   jax.ShapeDtypeStruct((batch, MB_WIDTH, seq), F32), jax.ShapeDtypeStruct((batch, MB_WIDTH, seq), F32)],
        compiler_params=_params(("parallel", "parallel")),
        name="ffn_inproj_prompt",
    )(x, n1, wi, wo, nm, w_main, w_vt)


def _outproj_ffn_kernel(x_ref, ohg_ref, omb_ref, wout_ref, n2_ref, wi_ref, wo_ref, fn_ref, y_ref, *, final):
    x = x_ref[...]
    x = x + jnp.dot(ohg_ref[...], wout_ref[:HG_WIDTH, :], preferred_element_type=F32)
    x = x + jnp.dot(omb_ref[...], wout_ref[HG_WIDTH:, :], preferred_element_type=F32)
    x = _swiglu_half_step(x, n2_ref, wi_ref, wo_ref)
    if final:
        x = _rmsnorm(x, fn_ref[...])
    y_ref[...] = x


def _outproj_ffn(x, o_hg, o_mb, w_out, n2, wi, wo, fn, final):
    n = x.shape[0]
    rows = pl.BlockSpec((DENSE_ROWS, D_MODEL), lambda i: (i, 0))
    half = pl.BlockSpec((DENSE_ROWS, HG_WIDTH), lambda i: (i, 0))
    return pl.pallas_call(
        functools.partial(_outproj_ffn_kernel, final=final),
        grid=(n // DENSE_ROWS,),
        in_specs=[rows, half, half, _resident((D_MODEL, D_MODEL)), _resident((1, D_MODEL)),
                  _resident((D_MODEL, 2 * D_FF)), _resident((D_FF, D_MODEL)), _resident((1, D_MODEL))],
        out_specs=rows,
        out_shape=jax.ShapeDtypeStruct((n, D_MODEL), F32),
        compiler_params=_params(("parallel",)),
        name="outproj_ffn",
    )(x, o_hg, o_mb, w_out, n2, wi, wo, fn)


def _segment_products(f, top):
    rows = f.shape[0]
    t = lax.broadcasted_iota(jnp.int32, f.shape, 0)
    gq, gk, tot = f, jnp.ones_like(f), f
    out = {1: (gq, gk)}
    m = 1
    while m < top:
        second = (t & m) != 0
        up = pltpu.roll(tot, m, 0)
        dn = pltpu.roll(tot, rows - m, 0)
        gq = jnp.where(second, gq * up, gq)
        gk = jnp.where(second, gk, gk * dn)
        tot = tot * jnp.where(second, up, dn)
        m *= 2
        out[m] = (gq, gk)
    return out


def _intra_scores(q, k, prods, top):
    rows = q.shape[0]
    t = lax.broadcasted_iota(jnp.int32, (rows, rows), 0)
    s = lax.broadcasted_iota(jnp.int32, (rows, rows), 1)
    ts = t ^ s
    a = jnp.where(t == s, _dot_nt(q.astype(BF16), k.astype(BF16)), 0.0)
    m = 1
    while m < top:
        gq, gk = prods[m]
        p = _dot_nt((q * gq).astype(BF16), (k * gk).astype(BF16))
        a = jnp.where((ts >= m) & (ts < 2 * m) & (t > s), p, a)
        m *= 2
    return a


def _hgrn_inputs(hq, z, lb):
    sig = jax.nn.sigmoid(z)
    f = lb + (1.0 - lb) * sig
    k = (1.0 - lb) * jax.nn.sigmoid(-z)
    return _silu(hq), k, f


def _hgrn_finish(o, hg, gn):
    return _rmsnorm(o, gn) * _silu(hg)


def _hgrn_prompt_kernel(hq_ref, hf_ref, hi_ref, hg_ref, lb_ref, gn_ref, o_ref, s_ref, st_ref):
    step = pl.program_id(1)

    @pl.when(step == 0)
    def _():
        st_ref[...] = jnp.zeros_like(st_ref)

    def chunk_body(j, carry):
        r0 = pl.multiple_of(j * HG_CHUNK, HG_CHUNK)
        rs = pl.ds(r0, HG_CHUNK)
        for h in range(HG_HEADS):
            cs = slice(h * HG_D, (h + 1) * HG_D)
            q, k, f = _hgrn_inputs(hq_ref[rs, cs], hf_ref[rs, cs], lb_ref[:, cs])
            v = hi_ref[rs, cs].astype(BF16)
            prods = _segment_products(f, HG_CHUNK)
            a = _intra_scores(q, k, prods, HG_CHUNK)
            gq, gk = prods[HG_CHUNK]
            st = st_ref[h]
            o = jnp.dot(a.astype(BF16), v, preferred_element_type=F32)
            o = o + _dot_nt((q * gq).astype(BF16), st.astype(BF16))
            st_ref[h] = st * gq[HG_CHUNK - 1:HG_CHUNK, :] + _dot_tn(v, (k * gk).astype(BF16))
            o_ref[rs, cs] = _hgrn_finish(o, hg_ref[rs, cs], gn_ref[...]).astype(o_ref.dtype)
        return carry

    lax.fori_loop(0, HG_ROWS // HG_CHUNK, chunk_body, 0)

    @pl.when(step == pl.num_programs(1) - 1)
    def _():
        for h in range(HG_HEADS):
            s_ref[0, h] = st_ref[h].T


def _hgrn_prompt(proj, lb, gn, batch, seq):
    steps = seq // HG_ROWS

    def col(c):
        return pl.BlockSpec((HG_ROWS, HG_WIDTH), lambda b, i: (b * steps + i, c))

    return pl.pallas_call(
        _hgrn_prompt_kernel,
        grid=(batch, steps),
        in_specs=[col(0), col(1), col(2), col(3), _resident((1, HG_WIDTH)), _resident((1, HG_D))],
        out_specs=[pl.BlockSpec((HG_ROWS, HG_WIDTH), lambda b, i: (b * steps + i, 0)),
                   pl.BlockSpec((1, HG_HEADS, HG_D, HG_D), lambda b, i: (b, 0, 0, 0))],
        out_shape=[jax.ShapeDtypeStruct((batch * seq, HG_WIDTH), BF16),
                   jax.ShapeDtypeStruct((batch, HG_HEADS, HG_D, HG_D), F32)],
        scratch_shapes=[pltpu.VMEM((HG_HEADS, HG_D, HG_D), F32)],
        compiler_params=_params(("parallel", "arbitrary")),
        name="hgrn_prompt",
    )(proj, proj, proj, proj, lb, gn)


def _hgrn_decode_kernel(hq_ref, hf_ref, hi_ref, hg_ref, lb_ref, gn_ref, s0_ref, o_ref, s_ref, *, seq):
    rows = HG_DEC_SEQS * seq
    row = lax.broadcasted_iota(jnp.int32, (rows, HG_D), 0)
    for h in range(HG_HEADS):
        cs = slice(h * HG_D, (h + 1) * HG_D)
        q, k, f = _hgrn_inputs(hq_ref[:, cs], hf_ref[:, cs], lb_ref[:, cs])
        v = hi_ref[:, cs].astype(BF16)
        prods = _segment_products(f, seq)
        a = _intra_scores(q, k, prods, seq)
        gq, gk = prods[seq]
        qg, kg = q * gq, k * gk
        o = jnp.dot(a.astype(BF16), v, preferred_element_type=F32)
        for i in range(HG_DEC_SEQS):
            mine = (row >= i * seq) & (row < (i + 1) * seq)
            s0 = s0_ref[0, i, h]
            o = o + jnp.dot(jnp.where(mine, qg, 0.0).astype(BF16), s0.astype(BF16), preferred_element_type=F32)
            last = (i + 1) * seq - 1
            decay = jnp.broadcast_to(gq[last:last + 1, :], (HG_D, HG_D)).T
            s_ref[i, h] = s0 * decay + _dot_tn(jnp.where(mine, kg, 0.0).astype(BF16), v)
        o_ref[:, cs] = _hgrn_finish(o, hg_ref[:, cs], gn_ref[...]).astype(o_ref.dtype)


def _hgrn_decode(proj, lb, gn, state, layer, batch, seq):
    rows = HG_DEC_SEQS * seq

    def col(c):
        return pl.BlockSpec((rows, HG_WIDTH), lambda g: (g, c))

    return pl.pallas_call(
        functools.partial(_hgrn_decode_kernel, seq=seq),
        grid=(batch // HG_DEC_SEQS,),
        in_specs=[col(0), col(1), col(2), col(3), _resident((1, HG_WIDTH)), _resident((1, HG_D)),
                  pl.BlockSpec((1, HG_DEC_SEQS, HG_HEADS, HG_D, HG_D), lambda g: (layer, g, 0, 0, 0))],
        out_specs=[pl.BlockSpec((rows, HG_WIDTH), lambda g: (g, 0)),
                   pl.BlockSpec((HG_DEC_SEQS, HG_HEADS, HG_D, HG_D), lambda g: (g, 0, 0, 0))],
        out_shape=[jax.ShapeDtypeStruct((batch * seq, HG_WIDTH), BF16),
                   jax.ShapeDtypeStruct((batch, HG_HEADS, HG_D, HG_D), F32)],
        compiler_params=_params(("parallel",)),
        name="hgrn_decode",
    )(proj, proj, proj, proj, lb, gn, state)


def _top_blocks(gate, valid):
    lanef = lax.broadcasted_iota(jnp.int32, gate.shape, 1).astype(F32)
    g = jnp.where(valid, gate, NEG_INF)
    picks = []
    for _ in range(MB_TOPK):
        mx = jnp.max(g, axis=-1, keepdims=True)
        idx = jnp.min(jnp.where(g == mx, lanef, float(LANES)), axis=-1, keepdims=True)
        picks.append(jnp.where(mx > NEG_INF, idx, -1.0))
        g = jnp.where(lanef == idx, NEG_INF, g)
    return picks


def _picked_offsets_by_row(gate_t, n_valid):
    rowi = lax.broadcasted_iota(jnp.int32, gate_t.shape, 0)
    rowf = rowi.astype(F32)
    g = jnp.where(rowi < n_valid, gate_t, NEG_INF)
    offsets = jnp.full(gate_t.shape, MASKED, F32)
    for _ in range(MB_TOPK):
        mx = jnp.max(g, axis=0, keepdims=True)
        idx = jnp.min(jnp.where(g == mx, rowf, float(LANES)), axis=0, keepdims=True)
        hit = (rowf == idx) & (mx > NEG_INF)
        offsets = jnp.where(hit, 0.0, offsets)
        g = jnp.where(hit, NEG_INF, g)
    return offsets


def _moba_prompt_kernel(q_ref, qn_ref, k_ref, vt_ref, o_ref, ka_ref, kb_ref, va_ref, vb_ref, km_ref, s_ref, qsel_ref, *,
                        n_blocks):
    i = pl.program_id(2)
    lane = lax.broadcasted_iota(jnp.int32, (MB_BLOCK, LANES), 1)
    first = lane < MB_DH
    feat = lax.broadcasted_iota(jnp.int32, (LANES, MB_BLOCK), 0)
    top = feat < MB_DH

    @pl.when(i == 0)
    def _():
        qsel_ref[0] = jnp.zeros(qsel_ref.shape[1:], BF16)

        def prep(j, carry):
            rs = pl.ds(pl.multiple_of(j * MB_BLOCK, MB_BLOCK), MB_BLOCK)
            k = k_ref[rs, :]
            vt = vt_ref[0, :, rs]
            ka_ref[rs, :] = jnp.where(first, k, jnp.where(lane == MB_DH + j, 1.0, 0.0)).astype(BF16)
            kb_ref[rs, :] = jnp.where(first, jnp.where(lane == j, 1.0, 0.0), k).astype(BF16)
            va_ref[:, rs] = jnp.where(top, vt, 1.0).astype(BF16)
            vb_ref[:, rs] = jnp.where(top, 1.0, vt).astype(BF16)
            km_ref[pl.ds(j, 1), :] = jnp.mean(k, axis=0, keepdims=True)
            return carry

        lax.fori_loop(0, n_blocks, prep, 0)

    q = q_ref[...] * SCORE_SCALE
    heads = [(jnp.where(first, q, 0.0).astype(BF16), qsel_ref[i % 2, 0], ka_ref, va_ref),
             (jnp.where(first, 0.0, q).astype(BF16), qsel_ref[i % 2, 1], kb_ref, vb_ref)]
    v_rows = (slice(0, MB_DH + ONES_ROWS), slice(MB_DH - ONES_ROWS, LANES))

    def select_for_next_tile():
        q_next = qn_ref[...] * (MB_DH ** -0.5)
        for h, is_a in enumerate((True, False)):
            qh = jnp.where(first, q_next, 0.0) if is_a else jnp.where(first, 0.0, q_next)
            gate_t = _dot_nt(km_ref[...], qh, precision=lax.Precision.HIGHEST)
            offs = _picked_offsets_by_row(gate_t, i + 1)
            before = MB_DH if is_a else 0
            slots = ([jnp.zeros((before, MB_BLOCK), F32)] if before else []) + [offs]
            slots.append(jnp.zeros((LANES - before - n_blocks, MB_BLOCK), F32))
            sel = jnp.concatenate(slots, axis=0).T
            q_feat = q_next * LOG2_E
            q_sel = jnp.where(first, q_feat, sel) if is_a else jnp.where(first, sel, q_feat)
            qsel_ref[(i + 1) % 2, h] = q_sel.astype(BF16)

    group_rows = MB_GROUP * MB_BLOCK
    last_group = n_blocks // MB_GROUP - 1

    def group(g):
        return pl.ds(pl.multiple_of(g * group_rows, group_rows), group_rows)

    def produce(g, slot):
        rs = group(jnp.minimum(g, last_group))
        cmax = []
        for h, (_, q_sel, k_ref_h, _) in enumerate(heads):
            s = _dot_nt(k_ref_h[rs, :], q_sel)
            s_ref[slot, h] = s
            cmax.append(jnp.max(s, axis=0, keepdims=True))
        return cmax

    def consume(g, slot, cmax, state):
        rs = group(g)
        out = []
        for h, (_, _, _, v_ref_h) in enumerate(heads):
            m, acc = state[2 * h:2 * h + 2]
            m_new = jnp.maximum(m, cmax[h])
            p = jnp.exp2(s_ref[slot, h] - m_new).astype(BF16)
            out += [m_new, jnp.exp2(m - m_new) * acc + jnp.dot(v_ref_h[v_rows[h], rs], p, preferred_element_type=F32)]
        return out

    def two_groups(t, carry):
        cmax0, state = list(carry[:2]), list(carry[2:])
        cmax1 = produce(2 * t + 1, 1)
        state = consume(2 * t, 0, cmax0, state)
        cmax0 = produce(2 * t + 2, 0)
        state = consume(2 * t + 1, 1, cmax1, state)
        return tuple(cmax0 + state)

    select_for_next_tile()
    own = pl.ds(pl.multiple_of(i * MB_BLOCK, MB_BLOCK), MB_BLOCK)
    key = lax.broadcasted_iota(jnp.int32, (MB_BLOCK, MB_BLOCK), 0)
    qry = lax.broadcasted_iota(jnp.int32, (MB_BLOCK, MB_BLOCK), 1)
    own_scores = [jnp.where(key <= qry, _dot_nt(k_ref_h[own, :], q_own), NEG_INF) for q_own, _, k_ref_h, _ in heads]
    cmax0 = produce(0, 0)
    init = []
    for h, (s, (_, _, _, v_ref_h)) in enumerate(zip(own_scores, heads)):
        m = jnp.max(s, axis=0, keepdims=True)
        init += [m, jnp.dot(v_ref_h[v_rows[h], own], jnp.exp2(s - m).astype(BF16), preferred_element_type=F32)]
    res = lax.fori_loop(0, (i + 2 * MB_GROUP - 1) // (2 * MB_GROUP), two_groups, tuple(cmax0 + init))
    acc_a, acc_b = res[3], res[5]
    num = jnp.concatenate([acc_a[:MB_DH], acc_b[ONES_ROWS:]], axis=0)
    den = jnp.where(top, acc_a[MB_DH:MB_DH + 1, :], acc_b[:1, :])
    o_ref[...] = (num / den).T.astype(o_ref.dtype)


def _moba_prompt(main, vt, batch, seq):
    n_blocks = seq // MB_BLOCK
    assert n_blocks % 8 == 0 and n_blocks % (2 * MB_GROUP) == 0 and n_blocks <= MB_DH
    pairs = MB_WIDTH // LANES
    q0 = 4 * HG_WIDTH // LANES
    return pl.pallas_call(
        functools.partial(_moba_prompt_kernel, n_blocks=n_blocks),
        grid=(batch, pairs, n_blocks),
        in_specs=[pl.BlockSpec((MB_BLOCK, LANES), lambda b, p, i: (b * n_blocks + i, q0 + p)),
                  pl.BlockSpec((MB_BLOCK, LANES),
                               lambda b, p, i: (b * n_blocks + jnp.minimum(i + 1, n_blocks - 1), q0 + p)),
                  pl.BlockSpec((seq, LANES), lambda b, p, i: (b, q0 + pairs + p)),
                  pl.BlockSpec((1, LANES, seq), lambda b, p, i: (b, p, 0))],
        out_specs=pl.BlockSpec((MB_BLOCK, LANES), lambda b, p, i: (b * n_blocks + i, p)),
        out_shape=jax.ShapeDtypeStruct((batch * seq, MB_WIDTH), BF16),
        scratch_shapes=[pltpu.VMEM((seq, LANES), BF16)] * 2 + [pltpu.VMEM((LANES, seq), BF16)] * 2
        + [pltpu.VMEM((n_blocks, LANES), F32), pltpu.VMEM((2, 2, MB_GROUP * MB_BLOCK, MB_BLOCK), F32),
           pltpu.VMEM((2, 2, MB_BLOCK, LANES), BF16)],
        compiler_params=_params(("parallel", "parallel", "arbitrary")),
        name="moba_prompt",
    )(main, main, main, vt)


def _moba_decode_kernel(pt_ref, q_ref, k_ref, v_ref, *refs, n_pages, seq):
    del pt_ref
    k_pages, v_pages, o_ref = refs[:n_pages], refs[n_pages:2 * n_pages], refs[2 * n_pages]
    pages_per_block = MB_BLOCK // PAGE_SIZE
    n_blocks = n_pages // pages_per_block
    rows = seq * MB_HEADS
    sub = lax.broadcasted_iota(jnp.int32, (MB_HEADS, MB_WIDTH), 0)
    head_of_lane = lax.broadcasted_iota(jnp.int32, (MB_HEADS, MB_WIDTH), 1) // MB_DH
    head_mask = sub == head_of_lane
    big = lax.broadcasted_iota(jnp.int32, (LANES, MB_WIDTH), 0)

    def rows_to_tile(x):
        out = jnp.zeros((LANES, MB_WIDTH), F32)
        for t in range(x.shape[0]):
            out = jnp.where(big == t, jnp.broadcast_to(x[t:t + 1, :], out.shape), out)
        return out

    q = q_ref[0] * (MB_DH ** -0.5)
    qbd = jnp.concatenate(
        [jnp.where(head_mask, jnp.broadcast_to(q[t:t + 1, :], head_mask.shape), 0.0) for t in range(seq)], axis=0)
    qbd16 = qbd.astype(BF16)

    raw = [jnp.dot(qbd16, k_pages[p][0].astype(BF16), preferred_element_type=F32) for p in range(n_pages)]
    glane = lax.broadcasted_iota(jnp.int32, (rows, LANES), 1)
    r = lax.broadcasted_iota(jnp.int32, (rows, LANES), 0) // MB_HEADS
    own_scores = jnp.where(glane <= r, _dot_nt(qbd16, rows_to_tile(k_ref[0]).astype(BF16)), NEG_INF)

    lane = lax.broadcasted_iota(jnp.int32, (MB_WIDTH, LANES), 1)
    kmean_t = jnp.zeros((MB_WIDTH, LANES), F32)
    for j in range(n_blocks):
        tot = sum(k_pages[j * pages_per_block + u][0] for u in range(pages_per_block))
        kmean_t = jnp.where(lane == j, jnp.sum(tot, axis=1, keepdims=True) * (1.0 / MB_BLOCK), kmean_t)
    gate = jnp.dot(qbd, kmean_t, preferred_element_type=F32, precision=lax.Precision.HIGHEST)
    picks = _top_blocks(gate, glane < n_blocks)

    def block_bias(j):
        hit = (picks[0] == j) | (picks[1] == j) | (picks[2] == j)
        return jnp.where(hit, 0.0, NEG_INF)

    scores = [raw[p] + block_bias(float(p // pages_per_block)) for p in range(n_pages)] + [own_scores]

    m = functools.reduce(jnp.maximum, scores)
    m = jnp.max(m, axis=-1, keepdims=True)
    l = jnp.zeros((rows, 1), F32)
    acc = jnp.zeros((rows, MB_WIDTH), F32)
    for p, s in enumerate(scores):
        w = jnp.exp(s - m)
        l = l + jnp.sum(w, axis=-1, keepdims=True)
        if p < n_pages:
            acc = acc + _dot_nt(w.astype(BF16), v_pages[p][0].astype(BF16))
        else:
            acc = acc + jnp.dot(w.astype(BF16), rows_to_tile(v_ref[0]).astype(BF16), preferred_element_type=F32)
    acc = acc / l
    for t in range(seq):
        own = jnp.where(head_mask, acc[t * MB_HEADS:(t + 1) * MB_HEADS, :], 0.0)
        o_ref[0, t:t + 1, :] = jnp.sum(own, axis=0, keepdims=True)


def _moba_decode(proj_dec, cache_kt, cache_vt, page_table, layer, n_pool):
    batch, seq, _ = proj_dec.shape
    n_pages = page_table.shape[1]
    q0 = 4 * HG_WIDTH // MB_WIDTH

    def new(c):
        return pl.BlockSpec((1, seq, MB_WIDTH), lambda b, pt: (b, 0, q0 + c))

    def page(j):
        return pl.BlockSpec((1, MB_WIDTH, PAGE_SIZE), lambda b, pt: (layer * n_pool + pt[b * n_pages + j], 0, 0))

    pages = [page(j) for j in range(n_pages)]
    grid_spec = pltpu.PrefetchScalarGridSpec(
        num_scalar_prefetch=1,
        grid=(batch,),
        in_specs=[new(0), new(1), new(2)] + pages + pages,
        out_specs=pl.BlockSpec((1, seq, MB_WIDTH), lambda b, pt: (b, 0, 0)),
    )
    return pl.pallas_call(
        functools.partial(_moba_decode_kernel, n_pages=n_pages, seq=seq),
        grid_spec=grid_spec,
        out_shape=jax.ShapeDtypeStruct((batch, seq, MB_WIDTH), F32),
        compiler_params=_params(("parallel",)),
        name="moba_decode",
    )(page_table.reshape(-1), proj_dec, proj_dec, proj_dec, *([cache_kt] * n_pages), *([cache_vt] * n_pages))


def _lower_bounds(gamma):
    p = jax.nn.softmax(gamma.astype(F32), axis=0)
    return jnp.cumsum(p, axis=0) - p[:1]


def _tokens_last_to_heads(x_t, batch, seq):
    return jnp.transpose(x_t.reshape(batch, MB_HEADS, MB_DH, seq), (0, 3, 1, 2))


def kernel(x_prompt, x_sample, cache_k, cache_v, state_hgrn, page_table, hg_lower_bounds, w_in, w_out, hg_gnorm,
           norm_ffn1, ffn1_wi, ffn1_wo, norm_mix, norm_ffn2, ffn2_wi, ffn2_wo, final_norm):
    batch, seq, _ = x_prompt.shape
    dec_batch, dec_seq, _ = x_sample.shape
    depth, n_pool = cache_k.shape[:2]
    lower = _lower_bounds(hg_lower_bounds)
    xp = x_prompt.reshape(batch * seq, D_MODEL)
    xs = x_sample.reshape(dec_batch * dec_seq, D_MODEL)
    fn = final_norm.reshape(1, D_MODEL)
    cache_kt = jnp.transpose(cache_k, (0, 1, 3, 4, 2)).reshape(depth * n_pool, MB_WIDTH, PAGE_SIZE)
    cache_vt = jnp.transpose(cache_v, (0, 1, 3, 4, 2)).reshape(depth * n_pool, MB_WIDTH, PAGE_SIZE)
    kp, vp, sp, ks, vs, ss = [], [], [], [], [], []
    k0, v0 = D_IN - 2 * MB_WIDTH, D_IN - MB_WIDTH
    for l in range(depth):
        lb = lower[l].reshape(1, HG_WIDTH)
        gn = hg_gnorm[l].reshape(1, HG_D)
        n1, nm, n2 = (a[l].reshape(1, D_MODEL) for a in (norm_ffn1, norm_mix, norm_ffn2))
        wi1, wo1, wi2, wo2 = (a[l].astype(BF16) for a in (ffn1_wi, ffn1_wo, ffn2_wi, ffn2_wo))
        w_in_l, w_out_l = w_in[l].astype(BF16), w_out[l].astype(BF16)
        xp, main, kt, vt = _ffn_inproj_prompt(xp, n1, wi1, wo1, nm, w_in_l[:, :D_MAIN], w_in_l[:, v0:].T, batch, seq)
        o_hg, s_p = _hgrn_prompt(main, lb, gn, batch, seq)
        o_mb = _moba_prompt(main, vt, batch, seq)
        xp = _outproj_ffn(xp, o_hg, o_mb, w_out_l, n2, wi2, wo2, fn, final=(l == depth - 1))
        kp.append(_tokens_last_to_heads(kt, batch, seq))
        vp.append(_tokens_last_to_heads(vt, batch, seq))
        sp.append(s_p)

        xs, proj = _ffn_inproj(xs, n1, wi1, wo1, nm, w_in_l)
        proj_dec = proj.reshape(dec_batch, dec_seq, D_IN)
        o_hg, s_s = _hgrn_decode(proj, lb, gn, state_hgrn, l, dec_batch, dec_seq)
        o_mb = _moba_decode(proj_dec, cache_kt, cache_vt, page_table, l, n_pool)
        o_mb = o_mb.reshape(dec_batch * dec_seq, MB_WIDTH).astype(BF16)
        xs = _outproj_ffn(xs, o_hg, o_mb, w_out_l, n2, wi2, wo2, fn, final=(l == depth - 1))
        ks.append(proj_dec[:, :, k0:v0].reshape(dec_batch, dec_seq, MB_HEADS, MB_DH))
        vs.append(proj_dec[:, :, v0:].reshape(dec_batch, dec_seq, MB_HEADS, MB_DH))
        ss.append(s_s)
    return (xp.reshape(batch, seq, D_MODEL), xs.reshape(dec_batch, dec_seq, D_MODEL), jnp.stack(kp), jnp.stack(vp),
            jnp.stack(sp), jnp.stack(ks), jnp.stack(vs), jnp.stack(ss))
```

```python
import functools

import jax
import jax.numpy as jnp
from jax import lax
from jax.experimental import pallas as pl
from jax.experimental.pallas import tpu as pltpu

F32 = jnp.float32
BF16 = jnp.bfloat16

D_MODEL = 1024
HG_HEADS = 4
HG_D = 128
HG_WIDTH = HG_HEADS * HG_D
MB_HEADS = 8
MB_DH = 64
MB_WIDTH = MB_HEADS * MB_DH
MB_BLOCK = 256
MB_TOPK = 3
D_FF = 2816
D_IN = 4 * HG_WIDTH + 3 * MB_WIDTH
D_MAIN = 4 * HG_WIDTH + 2 * MB_WIDTH
PAGE_SIZE = 128
EPS = 1e-6
FFN_SCALE = 0.5

LANES = 128
MXU_WIDTH = 256
FF_SPLITS = (0, 6 * MXU_WIDTH, D_FF)
DENSE_ROWS = 256
HG_CHUNK = 128
HG_ROWS = 512
HG_DEC_SEQS = 8
MB_GROUP = 2
LOG2_E = 1.4426950408889634
SCORE_SCALE = MB_DH ** -0.5 * LOG2_E
ONES_ROWS = 16
MASKED = -(2.0 ** 30)
VMEM_LIMIT = 56 * 1024 * 1024

NEG_INF = float("-inf")


def _resident(shape):
    return pl.BlockSpec(shape, lambda *_: (0,) * len(shape), pipeline_mode=pl.Buffered(1))


def _params(semantics):
    return pltpu.CompilerParams(dimension_semantics=semantics, vmem_limit_bytes=VMEM_LIMIT)


def _rmsnorm(x, g):
    return x * lax.rsqrt(jnp.mean(x * x, axis=-1, keepdims=True) + EPS) * g


def _silu(x):
    return x * jax.nn.sigmoid(x)


def _dot_nt(a, b, **kw):
    return lax.dot_general(a, b, (((1,), (1,)), ((), ())), preferred_element_type=F32, **kw)


def _dot_tn(a, b, **kw):
    return lax.dot_general(a, b, (((0,), (0,)), ((), ())), preferred_element_type=F32, **kw)


def _swiglu_half_step(x, n_ref, wi_ref, wo_ref):
    h = _rmsnorm(x, n_ref[...]).astype(BF16)
    acc = jnp.zeros_like(x)
    for lo, hi in zip(FF_SPLITS[:-1], FF_SPLITS[1:]):
        a = jnp.dot(h, wi_ref[:, lo:hi], preferred_element_type=F32)
        b = jnp.dot(h, wi_ref[:, D_FF + lo:D_FF + hi], preferred_element_type=F32)
        g = (_silu(a) * b).astype(BF16)
        acc = acc + jnp.dot(g, wo_ref[lo:hi, :], preferred_element_type=F32)
    return x + FFN_SCALE * acc


def _ffn_inproj_kernel(x_ref, n1_ref, wi_ref, wo_ref, nm_ref, win_ref, x1_ref, proj_ref):
    x1 = _swiglu_half_step(x_ref[...], n1_ref, wi_ref, wo_ref)
    x1_ref[...] = x1
    h = _rmsnorm(x1, nm_ref[...]).astype(BF16)
    proj_ref[...] = jnp.dot(h, win_ref[...], preferred_element_type=F32)


def _ffn_inproj(x, n1, wi, wo, nm, w_in):
    n = x.shape[0]
    rows = pl.BlockSpec((DENSE_ROWS, D_MODEL), lambda i: (i, 0))
    return pl.pallas_call(
        _ffn_inproj_kernel,
        grid=(n // DENSE_ROWS,),
        in_specs=[rows, _resident((1, D_MODEL)), _resident((D_MODEL, 2 * D_FF)), _resident((D_FF, D_MODEL)),
                  _resident((1, D_MODEL)), _resident((D_MODEL, D_IN))],
        out_specs=[rows, pl.BlockSpec((DENSE_ROWS, D_IN), lambda i: (i, 0))],
        out_shape=[jax.ShapeDtypeStruct((n, D_MODEL), F32), jax.ShapeDtypeStruct((n, D_IN), F32)],
        compiler_params=_params(("parallel",)),
        name="ffn_inproj",
    )(x, n1, wi, wo, nm, w_in)


def _ffn_inproj_prompt_kernel(x_ref, n1_ref, wi_ref, wo_ref, nm_ref, wmain_ref, wvt_ref, x1_ref, main_ref, kt_ref, vt_ref):
    x1 = _swiglu_half_step(x_ref[...], n1_ref, wi_ref, wo_ref)
    x1_ref[...] = x1
    h = _rmsnorm(x1, nm_ref[...]).astype(BF16)
    main = jnp.dot(h, wmain_ref[...], preferred_element_type=F32)
    main_ref[...] = main
    kt_ref[0] = main[:, D_MAIN - MB_WIDTH:].T
    vt_ref[0] = _dot_nt(wvt_ref[...], h)


def _ffn_inproj_prompt(x, n1, wi, wo, nm, w_main, w_vt, batch, seq):
    steps = seq // DENSE_ROWS
    rows = pl.BlockSpec((DENSE_ROWS, D_MODEL), lambda b, i: (b * steps + i, 0))
    tr = pl.BlockSpec((1, MB_WIDTH, DENSE_ROWS), lambda b, i: (b, 0, i))
    return pl.pallas_call(
        _ffn_inproj_prompt_kernel,
        grid=(batch, steps),
        in_specs=[rows, _resident((1, D_MODEL)), _resident((D_MODEL, 2 * D_FF)), _resident((D_FF, D_MODEL)),
                  _resident((1, D_MODEL)), _resident((D_MODEL, D_MAIN)), _resident((MB_WIDTH, D_MODEL))],
        out_specs=[rows, pl.BlockSpec((DENSE_ROWS, D_MAIN), lambda b, i: (b * steps + i, 0)), tr, tr],
        out_shape=[jax.ShapeDtypeStruct((batch * seq, D_MODEL), F32), jax.ShapeDtypeStruct((batch * seq, D_MAIN), F32),
                   jax.ShapeDtypeStruct((batch, MB_WIDTH, seq), F32), jax.ShapeDtypeStruct((batch, MB_WIDTH, seq), F32)],
        compiler_params=_params(("parallel", "parallel")),
        name="ffn_inproj_prompt",
    )(x, n1, wi, wo, nm, w_main, w_vt)


def _outproj_ffn_kernel(x_ref, ohg_ref, omb_ref, wout_ref, n2_ref, wi_ref, wo_ref, fn_ref, y_ref, *, final):
    x = x_ref[...]
    x = x + jnp.dot(ohg_ref[...], wout_ref[:HG_WIDTH, :], preferred_element_type=F32)
    x = x + jnp.dot(omb_ref[...], wout_ref[HG_WIDTH:, :], preferred_element_type=F32)
    x = _swiglu_half_step(x, n2_ref, wi_ref, wo_ref)
    if final:
        x = _rmsnorm(x, fn_ref[...])
    y_ref[...] = x


def _outproj_ffn(x, o_hg, o_mb, w_out, n2, wi, wo, fn, final):
    n = x.shape[0]
    rows = pl.BlockSpec((DENSE_ROWS, D_MODEL), lambda i: (i, 0))
    half = pl.BlockSpec((DENSE_ROWS, HG_WIDTH), lambda i: (i, 0))
    return pl.pallas_call(
        functools.partial(_outproj_ffn_kernel, final=final),
        grid=(n // DENSE_ROWS,),
        in_specs=[rows, half, half, _resident((D_MODEL, D_MODEL)), _resident((1, D_MODEL)),
                  _resident((D_MODEL, 2 * D_FF)), _resident((D_FF, D_MODEL)), _resident((1, D_MODEL))],
        out_specs=rows,
        out_shape=jax.ShapeDtypeStruct((n, D_MODEL), F32),
        compiler_params=_params(("parallel",)),
        name="outproj_ffn",
    )(x, o_hg, o_mb, w_out, n2, wi, wo, fn)


def _segment_products(f, top):
    rows = f.shape[0]
    t = lax.broadcasted_iota(jnp.int32, f.shape, 0)
    gq, gk, tot = f, jnp.ones_like(f), f
    out = {1: (gq, gk)}
    m = 1
    while m < top:
        second = (t & m) != 0
        up = pltpu.roll(tot, m, 0)
        dn = pltpu.roll(tot, rows - m, 0)
        gq = jnp.where(second, gq * up, gq)
        gk = jnp.where(second, gk, gk * dn)
        tot = gq * gk
        m *= 2
        out[m] = (gq, gk)
    return out


def _intra_scores(q, k, prods, top):
    rows = q.shape[0]
    t = lax.broadcasted_iota(jnp.int32, (rows, rows), 0)
    s = lax.broadcasted_iota(jnp.int32, (rows, rows), 1)
    ts = t ^ s
    a = jnp.where(t == s, _dot_nt(q.astype(BF16), k.astype(BF16)), 0.0)
    m = 1
    while m < top:
        gq, gk = prods[m]
        p = _dot_nt((q * gq).astype(BF16), (k * gk).astype(BF16))
        a = jnp.where((ts >= m) & (ts < 2 * m) & (t > s), p, a)
        m *= 2
    return a


def _hgrn_inputs(hq, z, lb):
    sig = jax.nn.sigmoid(z)
    f = lb + (1.0 - lb) * sig
    k = (1.0 - lb) * jax.nn.sigmoid(-z)
    return _silu(hq), k, f


def _hgrn_finish(o, hg, gn):
    return _rmsnorm(o, gn) * _silu(hg)


def _hgrn_prompt_kernel(hq_ref, hf_ref, hi_ref, hg_ref, lb_ref, gn_ref, o_ref, s_ref, st_ref):
    step = pl.program_id(1)

    @pl.when(step == 0)
    def _():
        st_ref[...] = jnp.zeros_like(st_ref)

    def chunk_body(j, carry):
        r0 = pl.multiple_of(j * HG_CHUNK, HG_CHUNK)
        rs = pl.ds(r0, HG_CHUNK)
        for h in range(HG_HEADS):
            cs = slice(h * HG_D, (h + 1) * HG_D)
            q, k, f = _hgrn_inputs(hq_ref[rs, cs], hf_ref[rs, cs], lb_ref[:, cs])
            v = hi_ref[rs, cs].astype(BF16)
            prods = _segment_products(f, HG_CHUNK)
            a = _intra_scores(q, k, prods, HG_CHUNK)
            gq, gk = prods[HG_CHUNK]
            st = st_ref[h]
            o = jnp.dot(a.astype(BF16), v, preferred_element_type=F32)
            o = o + _dot_nt((q * gq).astype(BF16), st.astype(BF16))
            st_ref[h] = st * gq[HG_CHUNK - 1:HG_CHUNK, :] + _dot_tn(v, (k * gk).astype(BF16))
            o_ref[rs, cs] = _hgrn_finish(o, hg_ref[rs, cs], gn_ref[...]).astype(o_ref.dtype)
        return carry

    lax.fori_loop(0, HG_ROWS // HG_CHUNK, chunk_body, 0)

    @pl.when(step == pl.num_programs(1) - 1)
    def _():
        for h in range(HG_HEADS):
            s_ref[0, h] = st_ref[h].T


def _hgrn_prompt(proj, lb, gn, batch, seq):
    steps = seq // HG_ROWS

    def col(c):
        return pl.BlockSpec((HG_ROWS, HG_WIDTH), lambda b, i: (b * steps + i, c))

    return pl.pallas_call(
        _hgrn_prompt_kernel,
        grid=(batch, steps),
        in_specs=[col(0), col(1), col(2), col(3), _resident((1, HG_WIDTH)), _resident((1, HG_D))],
        out_specs=[pl.BlockSpec((HG_ROWS, HG_WIDTH), lambda b, i: (b * steps + i, 0)),
                   pl.BlockSpec((1, HG_HEADS, HG_D, HG_D), lambda b, i: (b, 0, 0, 0))],
        out_shape=[jax.ShapeDtypeStruct((batch * seq, HG_WIDTH), BF16),
                   jax.ShapeDtypeStruct((batch, HG_HEADS, HG_D, HG_D), F32)],
        scratch_shapes=[pltpu.VMEM((HG_HEADS, HG_D, HG_D), F32)],
        compiler_params=_params(("parallel", "arbitrary")),
        name="hgrn_prompt",
    )(proj, proj, proj, proj, lb, gn)


def _hgrn_decode_kernel(hq_ref, hf_ref, hi_ref, hg_ref, lb_ref, gn_ref, s0_ref, o_ref, s_ref, *, seq):
    rows = HG_DEC_SEQS * seq
    row = lax.broadcasted_iota(jnp.int32, (rows, HG_D), 0)
    for h in range(HG_HEADS):
        cs = slice(h * HG_D, (h + 1) * HG_D)
        q, k, f = _hgrn_inputs(hq_ref[:, cs], hf_ref[:, cs], lb_ref[:, cs])
        v = hi_ref[:, cs].astype(BF16)
        prods = _segment_products(f, seq)
        a = _intra_scores(q, k, prods, seq)
        gq, gk = prods[seq]
        qg, kg = q * gq, k * gk
        o = jnp.dot(a.astype(BF16), v, preferred_element_type=F32)
        for i in range(HG_DEC_SEQS):
            mine = (row >= i * seq) & (row < (i + 1) * seq)
            s0 = s0_ref[0, i, h]
            o = o + jnp.dot(jnp.where(mine, qg, 0.0).astype(BF16), s0.astype(BF16), preferred_element_type=F32)
            last = (i + 1) * seq - 1
            decay = jnp.broadcast_to(gq[last:last + 1, :], (HG_D, HG_D)).T
            s_ref[i, h] = s0 * decay + _dot_tn(jnp.where(mine, kg, 0.0).astype(BF16), v)
        o_ref[:, cs] = _hgrn_finish(o, hg_ref[:, cs], gn_ref[...]).astype(o_ref.dtype)


def _hgrn_decode(proj, lb, gn, state, layer, batch, seq):
    rows = HG_DEC_SEQS * seq

    def col(c):
        return pl.BlockSpec((rows, HG_WIDTH), lambda g: (g, c))

    return pl.pallas_call(
        functools.partial(_hgrn_decode_kernel, seq=seq),
        grid=(batch // HG_DEC_SEQS,),
        in_specs=[col(0), col(1), col(2), col(3), _resident((1, HG_WIDTH)), _resident((1, HG_D)),
                  pl.BlockSpec((1, HG_DEC_SEQS, HG_HEADS, HG_D, HG_D), lambda g: (layer, g, 0, 0, 0))],
        out_specs=[pl.BlockSpec((rows, HG_WIDTH), lambda g: (g, 0)),
                   pl.BlockSpec((HG_DEC_SEQS, HG_HEADS, HG_D, HG_D), lambda g: (g, 0, 0, 0))],
        out_shape=[jax.ShapeDtypeStruct((batch * seq, HG_WIDTH), BF16),
                   jax.ShapeDtypeStruct((batch, HG_HEADS, HG_D, HG_D), F32)],
        compiler_params=_params(("parallel",)),
        name="hgrn_decode",
    )(proj, proj, proj, proj, lb, gn, state)


def _top_columns(gates):
    n = len(gates)
    g = list(gates)
    picked = [None] * n
    for _ in range(MB_TOPK):
        mx = functools.reduce(jnp.maximum, g)
        idx = functools.reduce(jnp.minimum, [jnp.where(g[j] == mx, float(j), float(n)) for j in range(n)])
        for j in range(n):
            hit = (idx == float(j)) & (mx > NEG_INF)
            picked[j] = hit if picked[j] is None else picked[j] | hit
            g[j] = jnp.where(hit, NEG_INF, g[j])
    return picked


def _picked_offsets_by_row(gate_t, n_valid):
    rowi = lax.broadcasted_iota(jnp.int32, gate_t.shape, 0)
    rowf = rowi.astype(F32)
    g = jnp.where(rowi < n_valid, gate_t, NEG_INF)
    offsets = jnp.full(gate_t.shape, MASKED, F32)
    for _ in range(MB_TOPK):
        mx = jnp.max(g, axis=0, keepdims=True)
        idx = jnp.min(jnp.where(g == mx, rowf, float(LANES)), axis=0, keepdims=True)
        hit = (rowf == idx) & (mx > NEG_INF)
        offsets = jnp.where(hit, 0.0, offsets)
        g = jnp.where(hit, NEG_INF, g)
    return offsets


def _moba_prompt_kernel(q_ref, qn_ref, k_ref, vt_ref, o_ref, ka_ref, kb_ref, va_ref, vb_ref, km_ref, s_ref, qsel_ref, *,
                        n_blocks):
    i = pl.program_id(2)
    lane = lax.broadcasted_iota(jnp.int32, (MB_BLOCK, LANES), 1)
    first = lane < MB_DH
    feat = lax.broadcasted_iota(jnp.int32, (LANES, MB_BLOCK), 0)
    top = feat < MB_DH

    @pl.when(i == 0)
    def _():
        qsel_ref[0] = jnp.zeros(qsel_ref.shape[1:], BF16)

        def prep(j, carry):
            rs = pl.ds(pl.multiple_of(j * MB_BLOCK, MB_BLOCK), MB_BLOCK)
            k = k_ref[rs, :]
            vt = vt_ref[0, :, rs]
            ka_ref[rs, :] = jnp.where(first, k, jnp.where(lane == MB_DH + j, 1.0, 0.0)).astype(BF16)
            kb_ref[rs, :] = jnp.where(first, jnp.where(lane == j, 1.0, 0.0), k).astype(BF16)
            va_ref[:, rs] = jnp.where(top, vt, 1.0).astype(BF16)
            vb_ref[:, rs] = jnp.where(top, 1.0, vt).astype(BF16)
            km_ref[pl.ds(j, 1), :] = jnp.mean(k, axis=0, keepdims=True)
            return carry

        lax.fori_loop(0, n_blocks, prep, 0)

    q = q_ref[...] * SCORE_SCALE
    heads = [(jnp.where(first, q, 0.0).astype(BF16), qsel_ref[i % 2, 0], ka_ref, va_ref),
             (jnp.where(first, 0.0, q).astype(BF16), qsel_ref[i % 2, 1], kb_ref, vb_ref)]
    v_rows = (slice(0, MB_DH + ONES_ROWS), slice(MB_DH - ONES_ROWS, LANES))

    def select_for_next_tile():
        q_next = qn_ref[...] * (MB_DH ** -0.5)
        for h, is_a in enumerate((True, False)):
            qh = jnp.where(first, q_next, 0.0) if is_a else jnp.where(first, 0.0, q_next)
            gate_t = _dot_nt(km_ref[...], qh, precision=lax.Precision.HIGHEST)
            offs = _picked_offsets_by_row(gate_t, i + 1)
            before = MB_DH if is_a else 0
            slots = ([jnp.zeros((before, MB_BLOCK), F32)] if before else []) + [offs]
            slots.append(jnp.zeros((LANES - before - n_blocks, MB_BLOCK), F32))
            sel = jnp.concatenate(slots, axis=0).T
            q_feat = q_next * LOG2_E
            q_sel = jnp.where(first, q_feat, sel) if is_a else jnp.where(first, sel, q_feat)
            qsel_ref[(i + 1) % 2, h] = q_sel.astype(BF16)

    group_rows = MB_GROUP * MB_BLOCK
    last_group = n_blocks // MB_GROUP - 1

    def group(g):
        return pl.ds(pl.multiple_of(g * group_rows, group_rows), group_rows)

    def produce(g, slot):
        rs = group(jnp.minimum(g, last_group))
        cmax = []
        for h, (_, q_sel, k_ref_h, _) in enumerate(heads):
            s = _dot_nt(k_ref_h[rs, :], q_sel)
            s_ref[slot, h] = s
            cmax.append(jnp.max(s, axis=0, keepdims=True))
        return cmax

    def consume(g, slot, cmax, state):
        rs = group(g)
        out = []
        for h, (_, _, _, v_ref_h) in enumerate(heads):
            m, acc = state[2 * h:2 * h + 2]
            m_new = jnp.maximum(m, cmax[h])
            p = jnp.exp2(s_ref[slot, h] - m_new).astype(BF16)
            out += [m_new, jnp.exp2(m - m_new) * acc + jnp.dot(v_ref_h[v_rows[h], rs], p, preferred_element_type=F32)]
        return out

    def two_groups(t, carry):
        cmax0, state = list(carry[:2]), list(carry[2:])
        cmax1 = produce(2 * t + 1, 1)
        state = consume(2 * t, 0, cmax0, state)
        cmax0 = produce(2 * t + 2, 0)
        state = consume(2 * t + 1, 1, cmax1, state)
        return tuple(cmax0 + state)

    select_for_next_tile()
    own = pl.ds(pl.multiple_of(i * MB_BLOCK, MB_BLOCK), MB_BLOCK)
    key = lax.broadcasted_iota(jnp.int32, (MB_BLOCK, MB_BLOCK), 0)
    qry = lax.broadcasted_iota(jnp.int32, (MB_BLOCK, MB_BLOCK), 1)
    own_scores = [jnp.where(key <= qry, _dot_nt(k_ref_h[own, :], q_own), NEG_INF) for q_own, _, k_ref_h, _ in heads]
    cmax0 = produce(0, 0)
    init = []
    for h, (s, (_, _, _, v_ref_h)) in enumerate(zip(own_scores, heads)):
        m = jnp.max(s, axis=0, keepdims=True)
        init += [m, jnp.dot(v_ref_h[v_rows[h], own], jnp.exp2(s - m).astype(BF16), preferred_element_type=F32)]
    res = lax.fori_loop(0, (i + 2 * MB_GROUP - 1) // (2 * MB_GROUP), two_groups, tuple(cmax0 + init))
    acc_a, acc_b = res[3], res[5]
    num = jnp.concatenate([acc_a[:MB_DH], acc_b[ONES_ROWS:]], axis=0)
    den = jnp.where(top, acc_a[MB_DH:MB_DH + 1, :], acc_b[:1, :])
    o_ref[...] = (num / den).T.astype(o_ref.dtype)


def _moba_prompt(main, vt, batch, seq):
    n_blocks = seq // MB_BLOCK
    assert n_blocks % 8 == 0 and n_blocks % (2 * MB_GROUP) == 0 and n_blocks <= MB_DH
    pairs = MB_WIDTH // LANES
    q0 = 4 * HG_WIDTH // LANES
    return pl.pallas_call(
        functools.partial(_moba_prompt_kernel, n_blocks=n_blocks),
        grid=(batch, pairs, n_blocks),
        in_specs=[pl.BlockSpec((MB_BLOCK, LANES), lambda b, p, i: (b * n_blocks + i, q0 + p)),
                  pl.BlockSpec((MB_BLOCK, LANES),
                               lambda b, p, i: (b * n_blocks + jnp.minimum(i + 1, n_blocks - 1), q0 + p)),
                  pl.BlockSpec((seq, LANES), lambda b, p, i: (b, q0 + pairs + p)),
                  pl.BlockSpec((1, LANES, seq), lambda b, p, i: (b, p, 0))],
        out_specs=pl.BlockSpec((MB_BLOCK, LANES), lambda b, p, i: (b * n_blocks + i, p)),
        out_shape=jax.ShapeDtypeStruct((batch * seq, MB_WIDTH), BF16),
        scratch_shapes=[pltpu.VMEM((seq, LANES), BF16)] * 2 + [pltpu.VMEM((LANES, seq), BF16)] * 2
        + [pltpu.VMEM((n_blocks, LANES), F32), pltpu.VMEM((2, 2, MB_GROUP * MB_BLOCK, MB_BLOCK), F32),
           pltpu.VMEM((2, 2, MB_BLOCK, LANES), BF16)],
        compiler_params=_params(("parallel", "parallel", "arbitrary")),
        name="moba_prompt",
    )(main, main, main, vt)


def _moba_decode_kernel(pt_ref, q_ref, k_ref, v_ref, *refs, n_pages, seq):
    del pt_ref
    k_pages, v_pages, o_ref = refs[:n_pages], refs[n_pages:2 * n_pages], refs[2 * n_pages]
    pages_per_block = MB_BLOCK // PAGE_SIZE
    n_blocks = n_pages // pages_per_block
    rows = seq * MB_HEADS
    sub = lax.broadcasted_iota(jnp.int32, (MB_HEADS, MB_WIDTH), 0)
    head_of_lane = lax.broadcasted_iota(jnp.int32, (MB_HEADS, MB_WIDTH), 1) // MB_DH
    head_mask = sub == head_of_lane
    big = lax.broadcasted_iota(jnp.int32, (LANES, MB_WIDTH), 0)

    def rows_to_tile(x):
        out = jnp.zeros((LANES, MB_WIDTH), F32)
        for t in range(x.shape[0]):
            out = jnp.where(big == t, jnp.broadcast_to(x[t:t + 1, :], out.shape), out)
        return out

    q = q_ref[0] * (MB_DH ** -0.5)
    qbd = jnp.concatenate(
        [jnp.where(head_mask, jnp.broadcast_to(q[t:t + 1, :], head_mask.shape), 0.0) for t in range(seq)], axis=0)
    qbd16 = qbd.astype(BF16)

    raw = [jnp.dot(qbd16, k_pages[p][0].astype(BF16), preferred_element_type=F32) for p in range(n_pages)]
    glane = lax.broadcasted_iota(jnp.int32, (rows, LANES), 1)
    r = lax.broadcasted_iota(jnp.int32, (rows, LANES), 0) // MB_HEADS
    own_scores = jnp.where(glane <= r, _dot_nt(qbd16, rows_to_tile(k_ref[0]).astype(BF16)), NEG_INF)

    gates = []
    for j in range(n_blocks):
        tot = sum(raw[j * pages_per_block + u] for u in range(pages_per_block))
        gates.append(jnp.sum(tot, axis=1, keepdims=True) * (1.0 / MB_BLOCK))
    bias = [jnp.where(hit, 0.0, NEG_INF) for hit in _top_columns(gates)]
    scores = [raw[p] + bias[p // pages_per_block] for p in range(n_pages)] + [own_scores]

    m = functools.reduce(jnp.maximum, scores)
    m = jnp.max(m, axis=-1, keepdims=True)
    l = jnp.zeros((rows, 1), F32)
    acc = jnp.zeros((rows, MB_WIDTH), F32)
    for p, s in enumerate(scores):
        w = jnp.exp(s - m)
        l = l + jnp.sum(w, axis=-1, keepdims=True)
        if p < n_pages:
            acc = acc + _dot_nt(w.astype(BF16), v_pages[p][0].astype(BF16))
        else:
            acc = acc + jnp.dot(w.astype(BF16), rows_to_tile(v_ref[0]).astype(BF16), preferred_element_type=F32)
    acc = acc / l
    for t in range(seq):
        own = jnp.where(head_mask, acc[t * MB_HEADS:(t + 1) * MB_HEADS, :], 0.0)
        o_ref[0, t:t + 1, :] = jnp.sum(own, axis=0, keepdims=True)


def _moba_decode(proj_dec, cache_kt, cache_vt, page_table, layer, n_pool):
    batch, seq, _ = proj_dec.shape
    n_pages = page_table.shape[1]
    q0 = 4 * HG_WIDTH // MB_WIDTH

    def new(c):
        return pl.BlockSpec((1, seq, MB_WIDTH), lambda b, pt: (b, 0, q0 + c))

    def page(j):
        return pl.BlockSpec((1, MB_WIDTH, PAGE_SIZE), lambda b, pt: (layer * n_pool + pt[b * n_pages + j], 0, 0))

    pages = [page(j) for j in range(n_pages)]
    grid_spec = pltpu.PrefetchScalarGridSpec(
        num_scalar_prefetch=1,
        grid=(batch,),
        in_specs=[new(0), new(1), new(2)] + pages + pages,
        out_specs=pl.BlockSpec((1, seq, MB_WIDTH), lambda b, pt: (b, 0, 0)),
    )
    return pl.pallas_call(
        functools.partial(_moba_decode_kernel, n_pages=n_pages, seq=seq),
        grid_spec=grid_spec,
        out_shape=jax.ShapeDtypeStruct((batch, seq, MB_WIDTH), F32),
        compiler_params=_params(("parallel",)),
        name="moba_decode",
    )(page_table.reshape(-1), proj_dec, proj_dec, proj_dec, *([cache_kt] * n_pages), *([cache_vt] * n_pages))


def _lower_bounds(gamma):
    p = jax.nn.softmax(gamma.astype(F32), axis=0)
    return jnp.cumsum(p, axis=0) - p[:1]


def _tokens_last_to_heads(x_t, batch, seq):
    return jnp.transpose(x_t.reshape(batch, MB_HEADS, MB_DH, seq), (0, 3, 1, 2))


def kernel(x_prompt, x_sample, cache_k, cache_v, state_hgrn, page_table, hg_lower_bounds, w_in, w_out, hg_gnorm,
           norm_ffn1, ffn1_wi, ffn1_wo, norm_mix, norm_ffn2, ffn2_wi, ffn2_wo, final_norm):
    batch, seq, _ = x_prompt.shape
    dec_batch, dec_seq, _ = x_sample.shape
    depth, n_pool = cache_k.shape[:2]
    lower = _lower_bounds(hg_lower_bounds)
    xp = x_prompt.reshape(batch * seq, D_MODEL)
    xs = x_sample.reshape(dec_batch * dec_seq, D_MODEL)
    fn = final_norm.reshape(1, D_MODEL)
    cache_kt = jnp.transpose(cache_k, (0, 1, 3, 4, 2)).reshape(depth * n_pool, MB_WIDTH, PAGE_SIZE)
    cache_vt = jnp.transpose(cache_v, (0, 1, 3, 4, 2)).reshape(depth * n_pool, MB_WIDTH, PAGE_SIZE)
    kp, vp, sp, ks, vs, ss = [], [], [], [], [], []
    k0, v0 = D_IN - 2 * MB_WIDTH, D_IN - MB_WIDTH
    for l in range(depth):
        lb = lower[l].reshape(1, HG_WIDTH)
        gn = hg_gnorm[l].reshape(1, HG_D)
        n1, nm, n2 = (a[l].reshape(1, D_MODEL) for a in (norm_ffn1, norm_mix, norm_ffn2))
        wi1, wo1, wi2, wo2 = (a[l].astype(BF16) for a in (ffn1_wi, ffn1_wo, ffn2_wi, ffn2_wo))
        w_in_l, w_out_l = w_in[l].astype(BF16), w_out[l].astype(BF16)
        xp, main, kt, vt = _ffn_inproj_prompt(xp, n1, wi1, wo1, nm, w_in_l[:, :D_MAIN], w_in_l[:, v0:].T, batch, seq)
        o_hg, s_p = _hgrn_prompt(main, lb, gn, batch, seq)
        o_mb = _moba_prompt(main, vt, batch, seq)
        xp = _outproj_ffn(xp, o_hg, o_mb, w_out_l, n2, wi2, wo2, fn, final=(l == depth - 1))
        kp.append(_tokens_last_to_heads(kt, batch, seq))
        vp.append(_tokens_last_to_heads(vt, batch, seq))
        sp.append(s_p)

        xs, proj = _ffn_inproj(xs, n1, wi1, wo1, nm, w_in_l)
        proj_dec = proj.reshape(dec_batch, dec_seq, D_IN)
        o_hg, s_s = _hgrn_decode(proj, lb, gn, state_hgrn, l, dec_batch, dec_seq)
        o_mb = _moba_decode(proj_dec, cache_kt, cache_vt, page_table, l, n_pool)
        o_mb = o_mb.reshape(dec_batch * dec_seq, MB_WIDTH).astype(BF16)
        xs = _outproj_ffn(xs, o_hg, o_mb, w_out_l, n2, wi2, wo2, fn, final=(l == depth - 1))
        ks.append(proj_dec[:, :, k0:v0].reshape(dec_batch, dec_seq, MB_HEADS, MB_DH))
        vs.append(proj_dec[:, :, v0:].reshape(dec_batch, dec_seq, MB_HEADS, MB_DH))
        ss.append(s_s)
    return (xp.reshape(batch, seq, D_MODEL), xs.reshape(dec_batch, dec_seq, D_MODEL), jnp.stack(kp), jnp.stack(vp),
            jnp.stack(sp), jnp.stack(ks), jnp.stack(vs), jnp.stack(ss))
```

```python
import functools

import jax
import jax.numpy as jnp
from jax import lax
from jax.experimental import pallas as pl
from jax.experimental.pallas import tpu as pltpu

F32 = jnp.float32
BF16 = jnp.bfloat16

D_MODEL = 1024
HG_HEADS = 4
HG_D = 128
HG_WIDTH = HG_HEADS * HG_D
MB_HEADS = 8
MB_DH = 64
MB_WIDTH = MB_HEADS * MB_DH
MB_BLOCK = 256
MB_TOPK = 3
D_FF = 2816
D_IN = 4 * HG_WIDTH + 3 * MB_WIDTH
D_MAIN = 4 * HG_WIDTH + 2 * MB_WIDTH
PAGE_SIZE = 128
EPS = 1e-6
FFN_SCALE = 0.5

LANES = 128
MXU_WIDTH = 256
FF_SPLITS = (0, 6 * MXU_WIDTH, D_FF)
DENSE_ROWS = 256
HG_CHUNK = 128
HG_ROWS = 512
HG_DEC_SEQS = 8
MB_GROUP = 2
LOG2_E = 1.4426950408889634
SCORE_SCALE = MB_DH ** -0.5 * LOG2_E
ONES_ROWS = 16
MASKED = -(2.0 ** 30)
VMEM_LIMIT = 56 * 1024 * 1024

NEG_INF = float("-inf")


def _resident(shape):
    return pl.BlockSpec(shape, lambda *_: (0,) * len(shape), pipeline_mode=pl.Buffered(1))


def _params(semantics):
    return pltpu.CompilerParams(dimension_semantics=semantics, vmem_limit_bytes=VMEM_LIMIT)


def _rmsnorm(x, g):
    return x * lax.rsqrt(jnp.mean(x * x, axis=-1, keepdims=True) + EPS) * g


def _silu(x):
    return x * jax.nn.sigmoid(x)


def _dot_nt(a, b, **kw):
    return lax.dot_general(a, b, (((1,), (1,)), ((), ())), preferred_element_type=F32, **kw)


def _dot_tn(a, b, **kw):
    return lax.dot_general(a, b, (((0,), (0,)), ((), ())), preferred_element_type=F32, **kw)


def _swiglu_half_step(x, n_ref, wi_ref, wo_ref):
    h = _rmsnorm(x, n_ref[...]).astype(BF16)
    acc = jnp.zeros_like(x)
    for lo, hi in zip(FF_SPLITS[:-1], FF_SPLITS[1:]):
        a = jnp.dot(h, wi_ref[:, lo:hi], preferred_element_type=F32)
        b = jnp.dot(h, wi_ref[:, D_FF + lo:D_FF + hi], preferred_element_type=F32)
        g = (_silu(a) * b).astype(BF16)
        acc = acc + jnp.dot(g, wo_ref[lo:hi, :], preferred_element_type=F32)
    return x + FFN_SCALE * acc


def _ffn_inproj_kernel(x_ref, n1_ref, wi_ref, wo_ref, nm_ref, win_ref, x1_ref, proj_ref):
    x1 = _swiglu_half_step(x_ref[...], n1_ref, wi_ref, wo_ref)
    x1_ref[...] = x1
    h = _rmsnorm(x1, nm_ref[...]).astype(BF16)
    proj_ref[...] = jnp.dot(h, win_ref[...], preferred_element_type=F32)


def _ffn_inproj(x, n1, wi, wo, nm, w_in):
    n = x.shape[0]
    rows = pl.BlockSpec((DENSE_ROWS, D_MODEL), lambda i: (i, 0))
    return pl.pallas_call(
        _ffn_inproj_kernel,
        grid=(n // DENSE_ROWS,),
        in_specs=[rows, _resident((1, D_MODEL)), _resident((D_MODEL, 2 * D_FF)), _resident((D_FF, D_MODEL)),
                  _resident((1, D_MODEL)), _resident((D_MODEL, D_IN))],
        out_specs=[rows, pl.BlockSpec((DENSE_ROWS, D_IN), lambda i: (i, 0))],
        out_shape=[jax.ShapeDtypeStruct((n, D_MODEL), F32), jax.ShapeDtypeStruct((n, D_IN), F32)],
        compiler_params=_params(("parallel",)),
        name="ffn_inproj",
    )(x, n1, wi, wo, nm, w_in)


def _ffn_inproj_prompt_kernel(x_ref, n1_ref, wi_ref, wo_ref, nm_ref, wmain_ref, wvt_ref, *rest):
    x1_ref, main_ref, kt_ref, vt_ref = rest[-4:]
    x1 = _swiglu_half_step(x_ref[...], n1_ref, wi_ref, wo_ref)
    x1_ref[...] = x1
    h = _rmsnorm(x1, nm_ref[...]).astype(BF16)
    main = jnp.dot(h, wmain_ref[...], preferred_element_type=F32)
    main_ref[...] = main
    kt_ref[0, 0] = main[:, D_MAIN - MB_WIDTH:].T
    vt_ref[0, 0] = _dot_nt(wvt_ref[...], h)


def _ffn_inproj_prompt(x, n1, wi, wo, nm, w_main, w_vt, stacks, layer, depth, batch, seq):
    steps = seq // DENSE_ROWS
    rows = pl.BlockSpec((DENSE_ROWS, D_MODEL), lambda b, i: (b * steps + i, 0))
    tr = pl.BlockSpec((1, 1, MB_WIDTH, DENSE_ROWS), lambda b, i: (layer, b, 0, i))
    stack = jax.ShapeDtypeStruct((depth, batch, MB_WIDTH, seq), F32)
    prev = () if stacks is None else tuple(stacks)
    return pl.pallas_call(
        _ffn_inproj_prompt_kernel,
        grid=(batch, steps),
        in_specs=[rows, _resident((1, D_MODEL)), _resident((D_MODEL, 2 * D_FF)), _resident((D_FF, D_MODEL)),
                  _resident((1, D_MODEL)), _resident((D_MODEL, D_MAIN)), _resident((MB_WIDTH, D_MODEL))]
        + [pl.BlockSpec(memory_space=pl.ANY)] * len(prev),
        out_specs=[rows, pl.BlockSpec((DENSE_ROWS, D_MAIN), lambda b, i: (b * steps + i, 0)), tr, tr],
        out_shape=[jax.ShapeDtypeStruct((batch * seq, D_MODEL), F32), jax.ShapeDtypeStruct((batch * seq, D_MAIN), F32),
                   stack, stack],
        input_output_aliases={7: 2, 8: 3} if prev else {},
        compiler_params=_params(("parallel", "parallel")),
        name="ffn_inproj_prompt",
    )(x, n1, wi, wo, nm, w_main, w_vt, *prev)


def _outproj_ffn_kernel(x_ref, ohg_ref, omb_ref, wout_ref, n2_ref, wi_ref, wo_ref, fn_ref, y_ref, *, final):
    x = x_ref[...]
    x = x + jnp.dot(ohg_ref[...], wout_ref[:HG_WIDTH, :], preferred_element_type=F32)
    x = x + jnp.dot(omb_ref[...], wout_ref[HG_WIDTH:, :], preferred_element_type=F32)
    x = _swiglu_half_step(x, n2_ref, wi_ref, wo_ref)
    if final:
        x = _rmsnorm(x, fn_ref[...])
    y_ref[...] = x


def _outproj_ffn(x, o_hg, o_mb, w_out, n2, wi, wo, fn, final):
    n = x.shape[0]
    rows = pl.BlockSpec((DENSE_ROWS, D_MODEL), lambda i: (i, 0))
    half = pl.BlockSpec((DENSE_ROWS, HG_WIDTH), lambda i: (i, 0))
    return pl.pallas_call(
        functools.partial(_outproj_ffn_kernel, final=final),
        grid=(n // DENSE_ROWS,),
        in_specs=[rows, half, half, _resident((D_MODEL, D_MODEL)), _resident((1, D_MODEL)),
                  _resident((D_MODEL, 2 * D_FF)), _resident((D_FF, D_MODEL)), _resident((1, D_MODEL))],
        out_specs=rows,
        out_shape=jax.ShapeDtypeStruct((n, D_MODEL), F32),
        compiler_params=_params(("parallel",)),
        name="outproj_ffn",
    )(x, o_hg, o_mb, w_out, n2, wi, wo, fn)


def _segment_products(f, top):
    rows = f.shape[0]
    t = lax.broadcasted_iota(jnp.int32, f.shape, 0)
    gq, gk, tot = f, jnp.ones_like(f), f
    out = {1: (gq, gk)}
    m = 1
    while m < top:
        second = (t & m) != 0
        up = pltpu.roll(tot, m, 0)
        dn = pltpu.roll(tot, rows - m, 0)
        gq = jnp.where(second, gq * up, gq)
        gk = jnp.where(second, gk, gk * dn)
        tot = gq * gk
        m *= 2
        out[m] = (gq, gk)
    return out


def _intra_scores(q, k, prods, top):
    rows = q.shape[0]
    t = lax.broadcasted_iota(jnp.int32, (rows, rows), 0)
    s = lax.broadcasted_iota(jnp.int32, (rows, rows), 1)
    ts = t ^ s
    a = jnp.where(t == s, _dot_nt(q.astype(BF16), k.astype(BF16)), 0.0)
    m = 1
    while m < top:
        gq, gk = prods[m]
        p = _dot_nt((q * gq).astype(BF16), (k * gk).astype(BF16))
        a = jnp.where((ts >= m) & (ts < 2 * m) & (t > s), p, a)
        m *= 2
    return a


def _hgrn_inputs(hq, z, lb):
    sig = jax.nn.sigmoid(z)
    f = lb + (1.0 - lb) * sig
    k = (1.0 - lb) * jax.nn.sigmoid(-z)
    return _silu(hq), k, f


def _hgrn_finish(o, hg, gn):
    return _rmsnorm(o, gn) * _silu(hg)


def _hgrn_prompt_kernel(hq_ref, hf_ref, hi_ref, hg_ref, lb_ref, gn_ref, o_ref, s_ref, st_ref):
    step = pl.program_id(1)

    @pl.when(step == 0)
    def _():
        st_ref[...] = jnp.zeros_like(st_ref)

    def chunk_body(j, carry):
        r0 = pl.multiple_of(j * HG_CHUNK, HG_CHUNK)
        rs = pl.ds(r0, HG_CHUNK)
        for h in range(HG_HEADS):
            cs = slice(h * HG_D, (h + 1) * HG_D)
            q, k, f = _hgrn_inputs(hq_ref[rs, cs], hf_ref[rs, cs], lb_ref[:, cs])
            v = hi_ref[rs, cs].astype(BF16)
            prods = _segment_products(f, HG_CHUNK)
            a = _intra_scores(q, k, prods, HG_CHUNK)
            gq, gk = prods[HG_CHUNK]
            st = st_ref[h]
            o = jnp.dot(a.astype(BF16), v, preferred_element_type=F32)
            o = o + _dot_nt((q * gq).astype(BF16), st.astype(BF16))
            st_ref[h] = st * gq[HG_CHUNK - 1:HG_CHUNK, :] + _dot_tn(v, (k * gk).astype(BF16))
            o_ref[rs, cs] = _hgrn_finish(o, hg_ref[rs, cs], gn_ref[...]).astype(o_ref.dtype)
        return carry

    lax.fori_loop(0, HG_ROWS // HG_CHUNK, chunk_body, 0)

    @pl.when(step == pl.num_programs(1) - 1)
    def _():
        for h in range(HG_HEADS):
            s_ref[0, h] = st_ref[h].T


def _hgrn_prompt(proj, lb, gn, batch, seq):
    steps = seq // HG_ROWS

    def col(c):
        return pl.BlockSpec((HG_ROWS, HG_WIDTH), lambda b, i: (b * steps + i, c))

    return pl.pallas_call(
        _hgrn_prompt_kernel,
        grid=(batch, steps),
        in_specs=[col(0), col(1), col(2), col(3), _resident((1, HG_WIDTH)), _resident((1, HG_D))],
        out_specs=[pl.BlockSpec((HG_ROWS, HG_WIDTH), lambda b, i: (b * steps + i, 0)),
                   pl.BlockSpec((1, HG_HEADS, HG_D, HG_D), lambda b, i: (b, 0, 0, 0))],
        out_shape=[jax.ShapeDtypeStruct((batch * seq, HG_WIDTH), BF16),
                   jax.ShapeDtypeStruct((batch, HG_HEADS, HG_D, HG_D), F32)],
        scratch_shapes=[pltpu.VMEM((HG_HEADS, HG_D, HG_D), F32)],
        compiler_params=_params(("parallel", "arbitrary")),
        name="hgrn_prompt",
    )(proj, proj, proj, proj, lb, gn)


def _hgrn_decode_kernel(hq_ref, hf_ref, hi_ref, hg_ref, lb_ref, gn_ref, s0_ref, *rest, seq):
    o_ref, s_ref = rest[-2:]
    rows = HG_DEC_SEQS * seq
    row = lax.broadcasted_iota(jnp.int32, (rows, HG_D), 0)
    for h in range(HG_HEADS):
        cs = slice(h * HG_D, (h + 1) * HG_D)
        q, k, f = _hgrn_inputs(hq_ref[:, cs], hf_ref[:, cs], lb_ref[:, cs])
        v = hi_ref[:, cs].astype(BF16)
        prods = _segment_products(f, seq)
        a = _intra_scores(q, k, prods, seq)
        gq, gk = prods[seq]
        qg, kg = q * gq, k * gk
        o = jnp.dot(a.astype(BF16), v, preferred_element_type=F32)
        for i in range(HG_DEC_SEQS):
            mine = (row >= i * seq) & (row < (i + 1) * seq)
            s0 = s0_ref[0, i, h]
            o = o + jnp.dot(jnp.where(mine, qg, 0.0).astype(BF16), s0.astype(BF16), preferred_element_type=F32)
            last = (i + 1) * seq - 1
            decay = jnp.broadcast_to(gq[last:last + 1, :], (HG_D, HG_D)).T
            s_ref[0, i, h] = s0 * decay + _dot_tn(jnp.where(mine, kg, 0.0).astype(BF16), v)
        o_ref[:, cs] = _hgrn_finish(o, hg_ref[:, cs], gn_ref[...]).astype(o_ref.dtype)


def _hgrn_decode(proj, lb, gn, state, stack, layer, batch, seq):
    rows = HG_DEC_SEQS * seq
    prev = () if stack is None else (stack,)

    def col(c):
        return pl.BlockSpec((rows, HG_WIDTH), lambda g: (g, c))

    return pl.pallas_call(
        functools.partial(_hgrn_decode_kernel, seq=seq),
        grid=(batch // HG_DEC_SEQS,),
        in_specs=[col(0), col(1), col(2), col(3), _resident((1, HG_WIDTH)), _resident((1, HG_D)),
                  pl.BlockSpec((1, HG_DEC_SEQS, HG_HEADS, HG_D, HG_D), lambda g: (layer, g, 0, 0, 0))]
        + [pl.BlockSpec(memory_space=pl.ANY)] * len(prev),
        out_specs=[pl.BlockSpec((rows, HG_WIDTH), lambda g: (g, 0)),
                   pl.BlockSpec((1, HG_DEC_SEQS, HG_HEADS, HG_D, HG_D), lambda g: (layer, g, 0, 0, 0))],
        out_shape=[jax.ShapeDtypeStruct((batch * seq, HG_WIDTH), BF16), jax.ShapeDtypeStruct(state.shape, F32)],
        input_output_aliases={7: 1} if prev else {},
        compiler_params=_params(("parallel",)),
        name="hgrn_decode",
    )(proj, proj, proj, proj, lb, gn, state, *prev)


def _top_columns(gates):
    n = len(gates)
    g = list(gates)
    picked = [None] * n
    for _ in range(MB_TOPK):
        mx = functools.reduce(jnp.maximum, g)
        idx = functools.reduce(jnp.minimum, [jnp.where(g[j] == mx, float(j), float(n)) for j in range(n)])
        for j in range(n):
            hit = (idx == float(j)) & (mx > NEG_INF)
            picked[j] = hit if picked[j] is None else picked[j] | hit
            g[j] = jnp.where(hit, NEG_INF, g[j])
    return picked


def _picked_offsets_by_row(gate_t, n_valid):
    rowi = lax.broadcasted_iota(jnp.int32, gate_t.shape, 0)
    rowf = rowi.astype(F32)
    g = jnp.where(rowi < n_valid, gate_t, NEG_INF)
    offsets = jnp.full(gate_t.shape, MASKED, F32)
    for _ in range(MB_TOPK):
        mx = jnp.max(g, axis=0, keepdims=True)
        idx = jnp.min(jnp.where(g == mx, rowf, float(LANES)), axis=0, keepdims=True)
        hit = (rowf == idx) & (mx > NEG_INF)
        offsets = jnp.where(hit, 0.0, offsets)
        g = jnp.where(hit, NEG_INF, g)
    return offsets


def _moba_prompt_kernel(q_ref, qn_ref, k_ref, vt_ref, o_ref, ka_ref, kb_ref, va_ref, vb_ref, km_ref, s_ref, qsel_ref, *,
                        n_blocks):
    i = pl.program_id(2)
    lane = lax.broadcasted_iota(jnp.int32, (MB_BLOCK, LANES), 1)
    first = lane < MB_DH
    feat = lax.broadcasted_iota(jnp.int32, (LANES, MB_BLOCK), 0)
    top = feat < MB_DH

    @pl.when(i == 0)
    def _():
        qsel_ref[0] = jnp.zeros(qsel_ref.shape[1:], BF16)

        def prep(j, carry):
            rs = pl.ds(pl.multiple_of(j * MB_BLOCK, MB_BLOCK), MB_BLOCK)
            k = k_ref[rs, :]
            vt = vt_ref[0, 0, :, rs]
            ka_ref[rs, :] = jnp.where(first, k, jnp.where(lane == MB_DH + j, 1.0, 0.0)).astype(BF16)
            kb_ref[rs, :] = jnp.where(first, jnp.where(lane == j, 1.0, 0.0), k).astype(BF16)
            va_ref[:, rs] = jnp.where(top, vt, 1.0).astype(BF16)
            vb_ref[:, rs] = jnp.where(top, 1.0, vt).astype(BF16)
            km_ref[pl.ds(j, 1), :] = jnp.mean(k, axis=0, keepdims=True)
            return carry

        lax.fori_loop(0, n_blocks, prep, 0)

    q = q_ref[...] * SCORE_SCALE
    heads = [(jnp.where(first, q, 0.0).astype(BF16), qsel_ref[i % 2, 0], ka_ref, va_ref),
             (jnp.where(first, 0.0, q).astype(BF16), qsel_ref[i % 2, 1], kb_ref, vb_ref)]
    v_rows = (slice(0, MB_DH + ONES_ROWS), slice(MB_DH - ONES_ROWS, LANES))

    def select_for_next_tile():
        q_next = qn_ref[...] * (MB_DH ** -0.5)
        for h, is_a in enumerate((True, False)):
            qh = jnp.where(first, q_next, 0.0) if is_a else jnp.where(first, 0.0, q_next)
            gate_t = _dot_nt(km_ref[...], qh, precision=lax.Precision.HIGHEST)
            offs = _picked_offsets_by_row(gate_t, i + 1)
            before = MB_DH if is_a else 0
            slots = ([jnp.zeros((before, MB_BLOCK), F32)] if before else []) + [offs]
            slots.append(jnp.zeros((LANES - before - n_blocks, MB_BLOCK), F32))
            sel = jnp.concatenate(slots, axis=0).T
            q_feat = q_next * LOG2_E
            q_sel = jnp.where(first, q_feat, sel) if is_a else jnp.where(first, sel, q_feat)
            qsel_ref[(i + 1) % 2, h] = q_sel.astype(BF16)

    group_rows = MB_GROUP * MB_BLOCK
    last_group = n_blocks // MB_GROUP - 1

    def group(g):
        return pl.ds(pl.multiple_of(g * group_rows, group_rows), group_rows)

    def produce(g, slot):
        rs = group(jnp.minimum(g, last_group))
        cmax = []
        for h, (_, q_sel, k_ref_h, _) in enumerate(heads):
            s = _dot_nt(k_ref_h[rs, :], q_sel)
            s_ref[slot, h] = s
            cmax.append(jnp.max(s, axis=0, keepdims=True))
        return cmax

    def consume(g, slot, cmax, state):
        rs = group(g)
        out = []
        for h, (_, _, _, v_ref_h) in enumerate(heads):
            m, acc = state[2 * h:2 * h + 2]
            m_new = jnp.maximum(m, cmax[h])
            p = jnp.exp2(s_ref[slot, h] - m_new).astype(BF16)
            out += [m_new, jnp.exp2(m - m_new) * acc + jnp.dot(v_ref_h[v_rows[h], rs], p, preferred_element_type=F32)]
        return out

    def two_groups(t, carry):
        cmax0, state = list(carry[:2]), list(carry[2:])
        cmax1 = produce(2 * t + 1, 1)
        state = consume(2 * t, 0, cmax0, state)
        cmax0 = produce(2 * t + 2, 0)
        state = consume(2 * t + 1, 1, cmax1, state)
        return tuple(cmax0 + state)

    select_for_next_tile()
    own = pl.ds(pl.multiple_of(i * MB_BLOCK, MB_BLOCK), MB_BLOCK)
    key = lax.broadcasted_iota(jnp.int32, (MB_BLOCK, MB_BLOCK), 0)
    qry = lax.broadcasted_iota(jnp.int32, (MB_BLOCK, MB_BLOCK), 1)
    own_scores = [jnp.where(key <= qry, _dot_nt(k_ref_h[own, :], q_own), NEG_INF) for q_own, _, k_ref_h, _ in heads]
    cmax0 = produce(0, 0)
    init = []
    for h, (s, (_, _, _, v_ref_h)) in enumerate(zip(own_scores, heads)):
        m = jnp.max(s, axis=0, keepdims=True)
        init += [m, jnp.dot(v_ref_h[v_rows[h], own], jnp.exp2(s - m).astype(BF16), preferred_element_type=F32)]
    res = lax.fori_loop(0, (i + 2 * MB_GROUP - 1) // (2 * MB_GROUP), two_groups, tuple(cmax0 + init))
    acc_a, acc_b = res[3], res[5]
    num = jnp.concatenate([acc_a[:MB_DH], acc_b[ONES_ROWS:]], axis=0)
    den = jnp.where(top, acc_a[MB_DH:MB_DH + 1, :], acc_b[:1, :])
    o_ref[...] = (num / den).T.astype(o_ref.dtype)


def _moba_prompt(main, vt_stack, layer, batch, seq):
    n_blocks = seq // MB_BLOCK
    assert n_blocks % 8 == 0 and n_blocks % (2 * MB_GROUP) == 0 and n_blocks <= MB_DH
    pairs = MB_WIDTH // LANES
    q0 = 4 * HG_WIDTH // LANES
    return pl.pallas_call(
        functools.partial(_moba_prompt_kernel, n_blocks=n_blocks),
        grid=(batch, pairs, n_blocks),
        in_specs=[pl.BlockSpec((MB_BLOCK, LANES), lambda b, p, i: (b * n_blocks + i, q0 + p)),
                  pl.BlockSpec((MB_BLOCK, LANES),
                               lambda b, p, i: (b * n_blocks + jnp.minimum(i + 1, n_blocks - 1), q0 + p)),
                  pl.BlockSpec((seq, LANES), lambda b, p, i: (b, q0 + pairs + p)),
                  pl.BlockSpec((1, 1, LANES, seq), lambda b, p, i: (layer, b, p, 0))],
        out_specs=pl.BlockSpec((MB_BLOCK, LANES), lambda b, p, i: (b * n_blocks + i, p)),
        out_shape=jax.ShapeDtypeStruct((batch * seq, MB_WIDTH), BF16),
        scratch_shapes=[pltpu.VMEM((seq, LANES), BF16)] * 2 + [pltpu.VMEM((LANES, seq), BF16)] * 2
        + [pltpu.VMEM((n_blocks, LANES), F32), pltpu.VMEM((2, 2, MB_GROUP * MB_BLOCK, MB_BLOCK), F32),
           pltpu.VMEM((2, 2, MB_BLOCK, LANES), BF16)],
        compiler_params=_params(("parallel", "parallel", "arbitrary")),
        name="moba_prompt",
    )(main, main, main, vt_stack)


def _moba_decode_kernel(pt_ref, q_ref, k_ref, v_ref, *refs, n_pages, seq):
    del pt_ref
    k_pages, v_pages, o_ref = refs[:n_pages], refs[n_pages:2 * n_pages], refs[2 * n_pages]
    pages_per_block = MB_BLOCK // PAGE_SIZE
    n_blocks = n_pages // pages_per_block
    rows = seq * MB_HEADS
    sub = lax.broadcasted_iota(jnp.int32, (MB_HEADS, MB_WIDTH), 0)
    head_of_lane = lax.broadcasted_iota(jnp.int32, (MB_HEADS, MB_WIDTH), 1) // MB_DH
    head_mask = sub == head_of_lane
    big = lax.broadcasted_iota(jnp.int32, (LANES, MB_WIDTH), 0)

    def rows_to_tile(x):
        out = jnp.zeros((LANES, MB_WIDTH), F32)
        for t in range(x.shape[0]):
            out = jnp.where(big == t, jnp.broadcast_to(x[t:t + 1, :], out.shape), out)
        return out

    q = q_ref[0] * (MB_DH ** -0.5)
    qbd = jnp.concatenate(
        [jnp.where(head_mask, jnp.broadcast_to(q[t:t + 1, :], head_mask.shape), 0.0) for t in range(seq)], axis=0)
    qbd16 = qbd.astype(BF16)

    raw = [jnp.dot(qbd16, k_pages[p][0].astype(BF16), preferred_element_type=F32) for p in range(n_pages)]
    glane = lax.broadcasted_iota(jnp.int32, (rows, LANES), 1)
    r = lax.broadcasted_iota(jnp.int32, (rows, LANES), 0) // MB_HEADS
    own_scores = jnp.where(glane <= r, _dot_nt(qbd16, rows_to_tile(k_ref[0]).astype(BF16)), NEG_INF)

    gates = []
    for j in range(n_blocks):
        tot = sum(raw[j * pages_per_block + u] for u in range(pages_per_block))
        gates.append(jnp.sum(tot, axis=1, keepdims=True) * (1.0 / MB_BLOCK))
    bias = [jnp.where(hit, 0.0, NEG_INF) for hit in _top_columns(gates)]
    scores = [raw[p] + bias[p // pages_per_block] for p in range(n_pages)] + [own_scores]

    m = functools.reduce(jnp.maximum, scores)
    m = jnp.max(m, axis=-1, keepdims=True)
    l = jnp.zeros((rows, 1), F32)
    acc = jnp.zeros((rows, MB_WIDTH), F32)
    for p, s in enumerate(scores):
        w = jnp.exp(s - m)
        l = l + jnp.sum(w, axis=-1, keepdims=True)
        if p < n_pages:
            acc = acc + _dot_nt(w.astype(BF16), v_pages[p][0].astype(BF16))
        else:
            acc = acc + jnp.dot(w.astype(BF16), rows_to_tile(v_ref[0]).astype(BF16), preferred_element_type=F32)
    acc = acc / l
    for t in range(seq):
        own = jnp.where(head_mask, acc[t * MB_HEADS:(t + 1) * MB_HEADS, :], 0.0)
        o_ref[0, t:t + 1, :] = jnp.sum(own, axis=0, keepdims=True)


def _moba_decode(proj_dec, cache_kt, cache_vt, page_table, layer, n_pool):
    batch, seq, _ = proj_dec.shape
    n_pages = page_table.shape[1]
    q0 = 4 * HG_WIDTH // MB_WIDTH

    def new(c):
        return pl.BlockSpec((1, seq, MB_WIDTH), lambda b, pt: (b, 0, q0 + c))

    def page(j):
        return pl.BlockSpec((1, MB_WIDTH, PAGE_SIZE), lambda b, pt: (layer * n_pool + pt[b * n_pages + j], 0, 0))

    pages = [page(j) for j in range(n_pages)]
    grid_spec = pltpu.PrefetchScalarGridSpec(
        num_scalar_prefetch=1,
        grid=(batch,),
        in_specs=[new(0), new(1), new(2)] + pages + pages,
        out_specs=pl.BlockSpec((1, seq, MB_WIDTH), lambda b, pt: (b, 0, 0)),
    )
    return pl.pallas_call(
        functools.partial(_moba_decode_kernel, n_pages=n_pages, seq=seq),
        grid_spec=grid_spec,
        out_shape=jax.ShapeDtypeStruct((batch, seq, MB_WIDTH), F32),
        compiler_params=_params(("parallel",)),
        name="moba_decode",
    )(page_table.reshape(-1), proj_dec, proj_dec, proj_dec, *([cache_kt] * n_pages), *([cache_vt] * n_pages))


def _lower_bounds(gamma):
    p = jax.nn.softmax(gamma.astype(F32), axis=0)
    return jnp.cumsum(p, axis=0) - p[:1]


def _tokens_last_to_heads(x_t, depth, batch, seq):
    return jnp.transpose(x_t.reshape(depth, batch, MB_HEADS, MB_DH, seq), (0, 1, 4, 2, 3))


def kernel(x_prompt, x_sample, cache_k, cache_v, state_hgrn, page_table, hg_lower_bounds, w_in, w_out, hg_gnorm,
           norm_ffn1, ffn1_wi, ffn1_wo, norm_mix, norm_ffn2, ffn2_wi, ffn2_wo, final_norm):
    batch, seq, _ = x_prompt.shape
    dec_batch, dec_seq, _ = x_sample.shape
    depth, n_pool = cache_k.shape[:2]
    lower = _lower_bounds(hg_lower_bounds)
    xp = x_prompt.reshape(batch * seq, D_MODEL)
    xs = x_sample.reshape(dec_batch * dec_seq, D_MODEL)
    fn = final_norm.reshape(1, D_MODEL)
    cache_kt = jnp.transpose(cache_k, (0, 1, 3, 4, 2)).reshape(depth * n_pool, MB_WIDTH, PAGE_SIZE)
    cache_vt = jnp.transpose(cache_v, (0, 1, 3, 4, 2)).reshape(depth * n_pool, MB_WIDTH, PAGE_SIZE)
    sp, ks, vs = [], [], []
    kv_stacks, state_stack = None, None
    k0, v0 = D_IN - 2 * MB_WIDTH, D_IN - MB_WIDTH
    for l in range(depth):
        lb = lower[l].reshape(1, HG_WIDTH)
        gn = hg_gnorm[l].reshape(1, HG_D)
        n1, nm, n2 = (a[l].reshape(1, D_MODEL) for a in (norm_ffn1, norm_mix, norm_ffn2))
        wi1, wo1, wi2, wo2 = (a[l].astype(BF16) for a in (ffn1_wi, ffn1_wo, ffn2_wi, ffn2_wo))
        w_in_l, w_out_l = w_in[l].astype(BF16), w_out[l].astype(BF16)
        xp, main, *kv_stacks = _ffn_inproj_prompt(xp, n1, wi1, wo1, nm, w_in_l[:, :D_MAIN], w_in_l[:, v0:].T, kv_stacks, l,
                                                  depth, batch, seq)
        o_hg, s_p = _hgrn_prompt(main, lb, gn, batch, seq)
        o_mb = _moba_prompt(main, kv_stacks[1], l, batch, seq)
        xp = _outproj_ffn(xp, o_hg, o_mb, w_out_l, n2, wi2, wo2, fn, final=(l == depth - 1))
        sp.append(s_p)

        xs, proj = _ffn_inproj(xs, n1, wi1, wo1, nm, w_in_l)
        proj_dec = proj.reshape(dec_batch, dec_seq, D_IN)
        o_hg, state_stack = _hgrn_decode(proj, lb, gn, state_hgrn, state_stack, l, dec_batch, dec_seq)
        o_mb = _moba_decode(proj_dec, cache_kt, cache_vt, page_table, l, n_pool)
        o_mb = o_mb.reshape(dec_batch * dec_seq, MB_WIDTH).astype(BF16)
        xs = _outproj_ffn(xs, o_hg, o_mb, w_out_l, n2, wi2, wo2, fn, final=(l == depth - 1))
        ks.append(proj_dec[:, :, k0:v0].reshape(dec_batch, dec_seq, MB_HEADS, MB_DH))
        vs.append(proj_dec[:, :, v0:].reshape(dec_batch, dec_seq, MB_HEADS, MB_DH))
    kp, vp = (_tokens_last_to_heads(a, depth, batch, seq) for a in kv_stacks)
    return (xp.reshape(batch, seq, D_MODEL), xs.reshape(dec_batch, dec_seq, D_MODEL), kp, vp,
            jnp.stack(sp), jnp.stack(ks), jnp.stack(vs), state_stack)
```

```python
import functools

import jax
import jax.numpy as jnp
from jax import lax
from jax.experimental import pallas as pl
from jax.experimental.pallas import tpu as pltpu

F32 = jnp.float32
BF16 = jnp.bfloat16

D_MODEL = 1024
HG_HEADS = 4
HG_D = 128
HG_WIDTH = HG_HEADS * HG_D
MB_HEADS = 8
MB_DH = 64
MB_WIDTH = MB_HEADS * MB_DH
MB_BLOCK = 256
MB_TOPK = 3
D_FF = 2816
D_IN = 4 * HG_WIDTH + 3 * MB_WIDTH
D_MAIN = 4 * HG_WIDTH + 2 * MB_WIDTH
PAGE_SIZE = 128
EPS = 1e-6
FFN_SCALE = 0.5

LANES = 128
MXU_WIDTH = 256
FF_SPLITS = (0, 6 * MXU_WIDTH, D_FF)
DENSE_ROWS = 256
HG_CHUNK = 128
HG_ROWS = 512
HG_DEC_SEQS = 8
MB_GROUP = 2
LOG2_E = 1.4426950408889634
SCORE_SCALE = MB_DH ** -0.5 * LOG2_E
ONES_ROWS = 16
MASKED = -(2.0 ** 30)
VMEM_LIMIT = 56 * 1024 * 1024

NEG_INF = float("-inf")


def _resident(shape):
    return pl.BlockSpec(shape, lambda *_: (0,) * len(shape), pipeline_mode=pl.Buffered(1))


def _layer_resident(shape, layer):
    return pl.BlockSpec((1,) + shape, lambda *_: (layer,) + (0,) * len(shape), pipeline_mode=pl.Buffered(1))


def _params(semantics):
    return pltpu.CompilerParams(dimension_semantics=semantics, vmem_limit_bytes=VMEM_LIMIT)


def _rmsnorm(x, g):
    return x * lax.rsqrt(jnp.mean(x * x, axis=-1, keepdims=True) + EPS) * g


def _silu(x):
    return x * jax.nn.sigmoid(x)


def _dot_nt(a, b, **kw):
    return lax.dot_general(a, b, (((1,), (1,)), ((), ())), preferred_element_type=F32, **kw)


def _dot_tn(a, b, **kw):
    return lax.dot_general(a, b, (((0,), (0,)), ((), ())), preferred_element_type=F32, **kw)


def _swiglu_half_step(x, n_ref, wi_ref, wo_ref):
    h = _rmsnorm(x, n_ref[...]).astype(BF16)
    acc = jnp.zeros_like(x)
    for lo, hi in zip(FF_SPLITS[:-1], FF_SPLITS[1:]):
        a = jnp.dot(h, wi_ref[0, :, lo:hi], preferred_element_type=F32)
        b = jnp.dot(h, wi_ref[0, :, D_FF + lo:D_FF + hi], preferred_element_type=F32)
        g = (_silu(a) * b).astype(BF16)
        acc = acc + jnp.dot(g, wo_ref[0, lo:hi, :], preferred_element_type=F32)
    return x + FFN_SCALE * acc


def _ffn_inproj_kernel(x_ref, n1_ref, wi_ref, wo_ref, nm_ref, win_ref, x1_ref, proj_ref):
    x1 = _swiglu_half_step(x_ref[...], n1_ref, wi_ref, wo_ref)
    x1_ref[...] = x1
    h = _rmsnorm(x1, nm_ref[...]).astype(BF16)
    proj_ref[...] = jnp.dot(h, win_ref[0], preferred_element_type=F32)


def _ffn_inproj(x, n1, wi, wo, nm, w_in, layer):
    n = x.shape[0]
    rows = pl.BlockSpec((DENSE_ROWS, D_MODEL), lambda i: (i, 0))
    return pl.pallas_call(
        _ffn_inproj_kernel,
        grid=(n // DENSE_ROWS,),
        in_specs=[rows, _resident((1, D_MODEL)), _layer_resident((D_MODEL, 2 * D_FF), layer),
                  _layer_resident((D_FF, D_MODEL), layer), _resident((1, D_MODEL)), _layer_resident((D_MODEL, D_IN), layer)],
        out_specs=[rows, pl.BlockSpec((DENSE_ROWS, D_IN), lambda i: (i, 0))],
        out_shape=[jax.ShapeDtypeStruct((n, D_MODEL), F32), jax.ShapeDtypeStruct((n, D_IN), F32)],
        compiler_params=_params(("parallel",)),
        name="ffn_inproj",
    )(x, n1, wi, wo, nm, w_in)


def _ffn_inproj_prompt_kernel(x_ref, n1_ref, wi_ref, wo_ref, nm_ref, win_ref, wvt_ref, *rest):
    x1_ref, main_ref, kt_ref, vt_ref = rest[-4:]
    x1 = _swiglu_half_step(x_ref[...], n1_ref, wi_ref, wo_ref)
    x1_ref[...] = x1
    h = _rmsnorm(x1, nm_ref[...]).astype(BF16)
    main = jnp.dot(h, win_ref[0, :, :D_MAIN], preferred_element_type=F32)
    main_ref[...] = main
    kt_ref[0, 0] = main[:, D_MAIN - MB_WIDTH:].T
    vt_ref[0, 0] = _dot_nt(wvt_ref[...], h)


def _ffn_inproj_prompt(x, n1, wi, wo, nm, w_in, w_vt, stacks, layer, depth, batch, seq):
    steps = seq // DENSE_ROWS
    rows = pl.BlockSpec((DENSE_ROWS, D_MODEL), lambda b, i: (b * steps + i, 0))
    tr = pl.BlockSpec((1, 1, MB_WIDTH, DENSE_ROWS), lambda b, i: (layer, b, 0, i))
    stack = jax.ShapeDtypeStruct((depth, batch, MB_WIDTH, seq), F32)
    prev = () if stacks is None else tuple(stacks)
    return pl.pallas_call(
        _ffn_inproj_prompt_kernel,
        grid=(batch, steps),
        in_specs=[rows, _resident((1, D_MODEL)), _layer_resident((D_MODEL, 2 * D_FF), layer),
                  _layer_resident((D_FF, D_MODEL), layer), _resident((1, D_MODEL)), _layer_resident((D_MODEL, D_IN), layer),
                  _resident((MB_WIDTH, D_MODEL))]
        + [pl.BlockSpec(memory_space=pl.ANY)] * len(prev),
        out_specs=[rows, pl.BlockSpec((DENSE_ROWS, D_MAIN), lambda b, i: (b * steps + i, 0)), tr, tr],
        out_shape=[jax.ShapeDtypeStruct((batch * seq, D_MODEL), F32), jax.ShapeDtypeStruct((batch * seq, D_MAIN), F32),
                   stack, stack],
        input_output_aliases={7: 2, 8: 3} if prev else {},
        compiler_params=_params(("parallel", "parallel")),
        name="ffn_inproj_prompt",
    )(x, n1, wi, wo, nm, w_in, w_vt, *prev)


def _outproj_ffn_kernel(x_ref, ohg_ref, omb_ref, wout_ref, n2_ref, wi_ref, wo_ref, fn_ref, y_ref, *, final):
    x = x_ref[...]
    x = x + jnp.dot(ohg_ref[...], wout_ref[0, :HG_WIDTH, :], preferred_element_type=F32)
    x = x + jnp.dot(omb_ref[...], wout_ref[0, HG_WIDTH:, :], preferred_element_type=F32)
    x = _swiglu_half_step(x, n2_ref, wi_ref, wo_ref)
    if final:
        x = _rmsnorm(x, fn_ref[...])
    y_ref[...] = x


def _outproj_ffn(x, o_hg, o_mb, w_out, n2, wi, wo, fn, layer, final):
    n = x.shape[0]
    rows = pl.BlockSpec((DENSE_ROWS, D_MODEL), lambda i: (i, 0))
    half = pl.BlockSpec((DENSE_ROWS, HG_WIDTH), lambda i: (i, 0))
    return pl.pallas_call(
        functools.partial(_outproj_ffn_kernel, final=final),
        grid=(n // DENSE_ROWS,),
        in_specs=[rows, half, half, _layer_resident((D_MODEL, D_MODEL), layer), _resident((1, D_MODEL)),
                  _layer_resident((D_MODEL, 2 * D_FF), layer), _layer_resident((D_FF, D_MODEL), layer),
                  _resident((1, D_MODEL))],
        out_specs=rows,
        out_shape=jax.ShapeDtypeStruct((n, D_MODEL), F32),
        compiler_params=_params(("parallel",)),
        name="outproj_ffn",
    )(x, o_hg, o_mb, w_out, n2, wi, wo, fn)


def _segment_products(f, top):
    rows = f.shape[0]
    t = lax.broadcasted_iota(jnp.int32, f.shape, 0)
    gq, gk, tot = f, jnp.ones_like(f), f
    out = {1: (gq, gk)}
    m = 1
    while m < top:
        second = (t & m) != 0
        up = pltpu.roll(tot, m, 0)
        dn = pltpu.roll(tot, rows - m, 0)
        gq = jnp.where(second, gq * up, gq)
        gk = jnp.where(second, gk, gk * dn)
        tot = gq * gk
        m *= 2
        out[m] = (gq, gk)
    return out


def _intra_scores(q, k, prods, top):
    rows = q.shape[0]
    t = lax.broadcasted_iota(jnp.int32, (rows, rows), 0)
    s = lax.broadcasted_iota(jnp.int32, (rows, rows), 1)
    ts = t ^ s
    a = jnp.where(t == s, _dot_nt(q.astype(BF16), k.astype(BF16)), 0.0)
    m = 1
    while m < top:
        gq, gk = prods[m]
        p = _dot_nt((q * gq).astype(BF16), (k * gk).astype(BF16))
        a = jnp.where((ts >= m) & (ts < 2 * m) & (t > s), p, a)
        m *= 2
    return a


def _hgrn_inputs(hq, z, lb):
    sig = jax.nn.sigmoid(z)
    f = lb + (1.0 - lb) * sig
    k = (1.0 - lb) * jax.nn.sigmoid(-z)
    return _silu(hq), k, f


def _hgrn_finish(o, hg, gn):
    return _rmsnorm(o, gn) * _silu(hg)


def _hgrn_prompt_kernel(hq_ref, hf_ref, hi_ref, hg_ref, lb_ref, gn_ref, o_ref, s_ref, st_ref):
    step = pl.program_id(1)

    @pl.when(step == 0)
    def _():
        st_ref[...] = jnp.zeros_like(st_ref)

    def chunk_body(j, carry):
        r0 = pl.multiple_of(j * HG_CHUNK, HG_CHUNK)
        rs = pl.ds(r0, HG_CHUNK)
        for h in range(HG_HEADS):
            cs = slice(h * HG_D, (h + 1) * HG_D)
            q, k, f = _hgrn_inputs(hq_ref[rs, cs], hf_ref[rs, cs], lb_ref[:, cs])
            v = hi_ref[rs, cs].astype(BF16)
            prods = _segment_products(f, HG_CHUNK)
            a = _intra_scores(q, k, prods, HG_CHUNK)
            gq, gk = prods[HG_CHUNK]
            st = st_ref[h]
            o = jnp.dot(a.astype(BF16), v, preferred_element_type=F32)
            o = o + _dot_nt((q * gq).astype(BF16), st.astype(BF16))
            st_ref[h] = st * gq[HG_CHUNK - 1:HG_CHUNK, :] + _dot_tn(v, (k * gk).astype(BF16))
            o_ref[rs, cs] = _hgrn_finish(o, hg_ref[rs, cs], gn_ref[...]).astype(o_ref.dtype)
        return carry

    lax.fori_loop(0, HG_ROWS // HG_CHUNK, chunk_body, 0)

    @pl.when(step == pl.num_programs(1) - 1)
    def _():
        for h in range(HG_HEADS):
            s_ref[0, h] = st_ref[h].T


def _hgrn_prompt(proj, lb, gn, batch, seq):
    steps = seq // HG_ROWS

    def col(c):
        return pl.BlockSpec((HG_ROWS, HG_WIDTH), lambda b, i: (b * steps + i, c))

    return pl.pallas_call(
        _hgrn_prompt_kernel,
        grid=(batch, steps),
        in_specs=[col(0), col(1), col(2), col(3), _resident((1, HG_WIDTH)), _resident((1, HG_D))],
        out_specs=[pl.BlockSpec((HG_ROWS, HG_WIDTH), lambda b, i: (b * steps + i, 0)),
                   pl.BlockSpec((1, HG_HEADS, HG_D, HG_D), lambda b, i: (b, 0, 0, 0))],
        out_shape=[jax.ShapeDtypeStruct((batch * seq, HG_WIDTH), BF16),
                   jax.ShapeDtypeStruct((batch, HG_HEADS, HG_D, HG_D), F32)],
        scratch_shapes=[pltpu.VMEM((HG_HEADS, HG_D, HG_D), F32)],
        compiler_params=_params(("parallel", "arbitrary")),
        name="hgrn_prompt",
    )(proj, proj, proj, proj, lb, gn)


def _hgrn_decode_kernel(hq_ref, hf_ref, hi_ref, hg_ref, lb_ref, gn_ref, s0_ref, *rest, seq):
    o_ref, s_ref = rest[-2:]
    rows = HG_DEC_SEQS * seq
    row = lax.broadcasted_iota(jnp.int32, (rows, HG_D), 0)
    for h in range(HG_HEADS):
        cs = slice(h * HG_D, (h + 1) * HG_D)
        q, k, f = _hgrn_inputs(hq_ref[:, cs], hf_ref[:, cs], lb_ref[:, cs])
        v = hi_ref[:, cs].astype(BF16)
        prods = _segment_products(f, seq)
        a = _intra_scores(q, k, prods, seq)
        gq, gk = prods[seq]
        qg, kg = q * gq, k * gk
        o = jnp.dot(a.astype(BF16), v, preferred_element_type=F32)
        for i in range(HG_DEC_SEQS):
            mine = (row >= i * seq) & (row < (i + 1) * seq)
            s0 = s0_ref[0, i, h]
            o = o + jnp.dot(jnp.where(mine, qg, 0.0).astype(BF16), s0.astype(BF16), preferred_element_type=F32)
            last = (i + 1) * seq - 1
            decay = jnp.broadcast_to(gq[last:last + 1, :], (HG_D, HG_D)).T
            s_ref[0, i, h] = s0 * decay + _dot_tn(jnp.where(mine, kg, 0.0).astype(BF16), v)
        o_ref[:, cs] = _hgrn_finish(o, hg_ref[:, cs], gn_ref[...]).astype(o_ref.dtype)


def _hgrn_decode(proj, lb, gn, state, stack, layer, batch, seq):
    rows = HG_DEC_SEQS * seq
    prev = () if stack is None else (stack,)

    def col(c):
        return pl.BlockSpec((rows, HG_WIDTH), lambda g: (g, c))

    return pl.pallas_call(
        functools.partial(_hgrn_decode_kernel, seq=seq),
        grid=(batch // HG_DEC_SEQS,),
        in_specs=[col(0), col(1), col(2), col(3), _resident((1, HG_WIDTH)), _resident((1, HG_D)),
                  pl.BlockSpec((1, HG_DEC_SEQS, HG_HEADS, HG_D, HG_D), lambda g: (layer, g, 0, 0, 0))]
        + [pl.BlockSpec(memory_space=pl.ANY)] * len(prev),
        out_specs=[pl.BlockSpec((rows, HG_WIDTH), lambda g: (g, 0)),
                   pl.BlockSpec((1, HG_DEC_SEQS, HG_HEADS, HG_D, HG_D), lambda g: (layer, g, 0, 0, 0))],
        out_shape=[jax.ShapeDtypeStruct((batch * seq, HG_WIDTH), BF16), jax.ShapeDtypeStruct(state.shape, F32)],
        input_output_aliases={7: 1} if prev else {},
        compiler_params=_params(("parallel",)),
        name="hgrn_decode",
    )(proj, proj, proj, proj, lb, gn, state, *prev)


def _top_columns(gates):
    n = len(gates)
    g = list(gates)
    picked = [None] * n
    for _ in range(MB_TOPK):
        mx = functools.reduce(jnp.maximum, g)
        idx = functools.reduce(jnp.minimum, [jnp.where(g[j] == mx, float(j), float(n)) for j in range(n)])
        for j in range(n):
            hit = (idx == float(j)) & (mx > NEG_INF)
            picked[j] = hit if picked[j] is None else picked[j] | hit
            g[j] = jnp.where(hit, NEG_INF, g[j])
    return picked


def _picked_offsets_by_row(gate_t, n_valid):
    rowi = lax.broadcasted_iota(jnp.int32, gate_t.shape, 0)
    rowf = rowi.astype(F32)
    g = jnp.where(rowi < n_valid, gate_t, NEG_INF)
    offsets = jnp.full(gate_t.shape, MASKED, F32)
    for _ in range(MB_TOPK):
        mx = jnp.max(g, axis=0, keepdims=True)
        idx = jnp.min(jnp.where(g == mx, rowf, float(LANES)), axis=0, keepdims=True)
        hit = (rowf == idx) & (mx > NEG_INF)
        offsets = jnp.where(hit, 0.0, offsets)
        g = jnp.where(hit, NEG_INF, g)
    return offsets


def _moba_prompt_kernel(q_ref, qn_ref, k_ref, vt_ref, o_ref, ka_ref, kb_ref, va_ref, vb_ref, km_ref, s_ref, qsel_ref, *,
                        n_blocks):
    i = pl.program_id(2)
    lane = lax.broadcasted_iota(jnp.int32, (MB_BLOCK, LANES), 1)
    first = lane < MB_DH
    feat = lax.broadcasted_iota(jnp.int32, (LANES, MB_BLOCK), 0)
    top = feat < MB_DH

    @pl.when(i == 0)
    def _():
        qsel_ref[0] = jnp.zeros(qsel_ref.shape[1:], BF16)

        def prep(j, carry):
            rs = pl.ds(pl.multiple_of(j * MB_BLOCK, MB_BLOCK), MB_BLOCK)
            k = k_ref[rs, :]
            vt = vt_ref[0, 0, :, rs]
            ka_ref[rs, :] = jnp.where(first, k, jnp.where(lane == MB_DH + j, 1.0, 0.0)).astype(BF16)
            kb_ref[rs, :] = jnp.where(first, jnp.where(lane == j, 1.0, 0.0), k).astype(BF16)
            va_ref[:, rs] = jnp.where(top, vt, 1.0).astype(BF16)
            vb_ref[:, rs] = jnp.where(top, 1.0, vt).astype(BF16)
            km_ref[pl.ds(j, 1), :] = jnp.mean(k, axis=0, keepdims=True)
            return carry

        lax.fori_loop(0, n_blocks, prep, 0)

    q = q_ref[...] * SCORE_SCALE
    heads = [(jnp.where(first, q, 0.0).astype(BF16), qsel_ref[i % 2, 0], ka_ref, va_ref),
             (jnp.where(first, 0.0, q).astype(BF16), qsel_ref[i % 2, 1], kb_ref, vb_ref)]
    v_rows = (slice(0, MB_DH + ONES_ROWS), slice(MB_DH - ONES_ROWS, LANES))

    def select_for_next_tile():
        q_next = qn_ref[...] * (MB_DH ** -0.5)
        for h, is_a in enumerate((True, False)):
            qh = jnp.where(first, q_next, 0.0) if is_a else jnp.where(first, 0.0, q_next)
            gate_t = _dot_nt(km_ref[...], qh, precision=lax.Precision.HIGHEST)
            offs = _picked_offsets_by_row(gate_t, i + 1)
            before = MB_DH if is_a else 0
            slots = ([jnp.zeros((before, MB_BLOCK), F32)] if before else []) + [offs]
            slots.append(jnp.zeros((LANES - before - n_blocks, MB_BLOCK), F32))
            sel = jnp.concatenate(slots, axis=0).T
            q_feat = q_next * LOG2_E
            q_sel = jnp.where(first, q_feat, sel) if is_a else jnp.where(first, sel, q_feat)
            qsel_ref[(i + 1) % 2, h] = q_sel.astype(BF16)

    group_rows = MB_GROUP * MB_BLOCK
    last_group = n_blocks // MB_GROUP - 1

    def group(g):
        return pl.ds(pl.multiple_of(g * group_rows, group_rows), group_rows)

    def produce(g, slot):
        rs = group(jnp.minimum(g, last_group))
        cmax = []
        for h, (_, q_sel, k_ref_h, _) in enumerate(heads):
            s = _dot_nt(k_ref_h[rs, :], q_sel)
            s_ref[slot, h] = s
            cmax.append(jnp.max(s, axis=0, keepdims=True))
        return cmax

    def consume(g, slot, cmax, state):
        rs = group(g)
        out = []
        for h, (_, _, _, v_ref_h) in enumerate(heads):
            m, acc = state[2 * h:2 * h + 2]
            m_new = jnp.maximum(m, cmax[h])
            p = jnp.exp2(s_ref[slot, h] - m_new).astype(BF16)
            out += [m_new, jnp.exp2(m - m_new) * acc + jnp.dot(v_ref_h[v_rows[h], rs], p, preferred_element_type=F32)]
        return out

    def two_groups(t, carry):
        cmax0, state = list(carry[:2]), list(carry[2:])
        cmax1 = produce(2 * t + 1, 1)
        state = consume(2 * t, 0, cmax0, state)
        cmax0 = produce(2 * t + 2, 0)
        state = consume(2 * t + 1, 1, cmax1, state)
        return tuple(cmax0 + state)

    select_for_next_tile()
    own = pl.ds(pl.multiple_of(i * MB_BLOCK, MB_BLOCK), MB_BLOCK)
    key = lax.broadcasted_iota(jnp.int32, (MB_BLOCK, MB_BLOCK), 0)
    qry = lax.broadcasted_iota(jnp.int32, (MB_BLOCK, MB_BLOCK), 1)
    own_scores = [jnp.where(key <= qry, _dot_nt(k_ref_h[own, :], q_own), NEG_INF) for q_own, _, k_ref_h, _ in heads]
    cmax0 = produce(0, 0)
    init = []
    for h, (s, (_, _, _, v_ref_h)) in enumerate(zip(own_scores, heads)):
        m = jnp.max(s, axis=0, keepdims=True)
        init += [m, jnp.dot(v_ref_h[v_rows[h], own], jnp.exp2(s - m).astype(BF16), preferred_element_type=F32)]
    res = lax.fori_loop(0, (i + 2 * MB_GROUP - 1) // (2 * MB_GROUP), two_groups, tuple(cmax0 + init))
    acc_a, acc_b = res[3], res[5]
    num = jnp.concatenate([acc_a[:MB_DH], acc_b[ONES_ROWS:]], axis=0)
    den = jnp.where(top, acc_a[MB_DH:MB_DH + 1, :], acc_b[:1, :])
    o_ref[...] = (num / den).T.astype(o_ref.dtype)


def _moba_prompt(main, vt_stack, layer, batch, seq):
    n_blocks = seq // MB_BLOCK
    assert n_blocks % 8 == 0 and n_blocks % (2 * MB_GROUP) == 0 and n_blocks <= MB_DH
    pairs = MB_WIDTH // LANES
    q0 = 4 * HG_WIDTH // LANES
    return pl.pallas_call(
        functools.partial(_moba_prompt_kernel, n_blocks=n_blocks),
        grid=(batch, pairs, n_blocks),
        in_specs=[pl.BlockSpec((MB_BLOCK, LANES), lambda b, p, i: (b * n_blocks + i, q0 + p)),
                  pl.BlockSpec((MB_BLOCK, LANES),
                               lambda b, p, i: (b * n_blocks + jnp.minimum(i + 1, n_blocks - 1), q0 + p)),
                  pl.BlockSpec((seq, LANES), lambda b, p, i: (b, q0 + pairs + p)),
                  pl.BlockSpec((1, 1, LANES, seq), lambda b, p, i: (layer, b, p, 0))],
        out_specs=pl.BlockSpec((MB_BLOCK, LANES), lambda b, p, i: (b * n_blocks + i, p)),
        out_shape=jax.ShapeDtypeStruct((batch * seq, MB_WIDTH), BF16),
        scratch_shapes=[pltpu.VMEM((seq, LANES), BF16)] * 2 + [pltpu.VMEM((LANES, seq), BF16)] * 2
        + [pltpu.VMEM((n_blocks, LANES), F32), pltpu.VMEM((2, 2, MB_GROUP * MB_BLOCK, MB_BLOCK), F32),
           pltpu.VMEM((2, 2, MB_BLOCK, LANES), BF16)],
        compiler_params=_params(("parallel", "parallel", "arbitrary")),
        name="moba_prompt",
    )(main, main, main, vt_stack)


def _moba_decode_kernel(pt_ref, q_ref, k_ref, v_ref, *refs, n_pages, seq):
    del pt_ref
    k_pages, v_pages, o_ref = refs[:n_pages], refs[n_pages:2 * n_pages], refs[2 * n_pages]
    pages_per_block = MB_BLOCK // PAGE_SIZE
    n_blocks = n_pages // pages_per_block
    rows = seq * MB_HEADS
    sub = lax.broadcasted_iota(jnp.int32, (MB_HEADS, MB_WIDTH), 0)
    head_of_lane = lax.broadcasted_iota(jnp.int32, (MB_HEADS, MB_WIDTH), 1) // MB_DH
    head_mask = sub == head_of_lane
    big = lax.broadcasted_iota(jnp.int32, (LANES, MB_WIDTH), 0)

    def rows_to_tile(x):
        out = jnp.zeros((LANES, MB_WIDTH), F32)
        for t in range(x.shape[0]):
            out = jnp.where(big == t, jnp.broadcast_to(x[t:t + 1, :], out.shape), out)
        return out

    q = q_ref[0] * (MB_DH ** -0.5)
    qbd = jnp.concatenate(
        [jnp.where(head_mask, jnp.broadcast_to(q[t:t + 1, :], head_mask.shape), 0.0) for t in range(seq)], axis=0)
    qbd16 = qbd.astype(BF16)

    raw = [jnp.dot(qbd16, k_pages[p][0].astype(BF16), preferred_element_type=F32) for p in range(n_pages)]
    glane = lax.broadcasted_iota(jnp.int32, (rows, LANES), 1)
    r = lax.broadcasted_iota(jnp.int32, (rows, LANES), 0) // MB_HEADS
    own_scores = jnp.where(glane <= r, _dot_nt(qbd16, rows_to_tile(k_ref[0]).astype(BF16)), NEG_INF)

    gates = []
    for j in range(n_blocks):
        tot = sum(raw[j * pages_per_block + u] for u in range(pages_per_block))
        gates.append(jnp.sum(tot, axis=1, keepdims=True) * (1.0 / MB_BLOCK))
    bias = [jnp.where(hit, 0.0, NEG_INF) for hit in _top_columns(gates)]
    scores = [raw[p] + bias[p // pages_per_block] for p in range(n_pages)] + [own_scores]

    m = functools.reduce(jnp.maximum, scores)
    m = jnp.max(m, axis=-1, keepdims=True)
    l = jnp.zeros((rows, 1), F32)
    acc = jnp.zeros((rows, MB_WIDTH), F32)
    for p, s in enumerate(scores):
        w = jnp.exp(s - m)
        l = l + jnp.sum(w, axis=-1, keepdims=True)
        if p < n_pages:
            acc = acc + _dot_nt(w.astype(BF16), v_pages[p][0].astype(BF16))
        else:
            acc = acc + jnp.dot(w.astype(BF16), rows_to_tile(v_ref[0]).astype(BF16), preferred_element_type=F32)
    acc = acc / l
    for t in range(seq):
        own = jnp.where(head_mask, acc[t * MB_HEADS:(t + 1) * MB_HEADS, :], 0.0)
        o_ref[0, t:t + 1, :] = jnp.sum(own, axis=0, keepdims=True)


def _moba_decode(proj_dec, cache_kt, cache_vt, page_table, layer, n_pool):
    batch, seq, _ = proj_dec.shape
    n_pages = page_table.shape[1]
    q0 = 4 * HG_WIDTH // MB_WIDTH

    def new(c):
        return pl.BlockSpec((1, seq, MB_WIDTH), lambda b, pt: (b, 0, q0 + c))

    def page(j):
        return pl.BlockSpec((1, MB_WIDTH, PAGE_SIZE), lambda b, pt: (layer * n_pool + pt[b * n_pages + j], 0, 0))

    pages = [page(j) for j in range(n_pages)]
    grid_spec = pltpu.PrefetchScalarGridSpec(
        num_scalar_prefetch=1,
        grid=(batch,),
        in_specs=[new(0), new(1), new(2)] + pages + pages,
        out_specs=pl.BlockSpec((1, seq, MB_WIDTH), lambda b, pt: (b, 0, 0)),
    )
    return pl.pallas_call(
        functools.partial(_moba_decode_kernel, n_pages=n_pages, seq=seq),
        grid_spec=grid_spec,
        out_shape=jax.ShapeDtypeStruct((batch, seq, MB_WIDTH), F32),
        compiler_params=_params(("parallel",)),
        name="moba_decode",
    )(page_table.reshape(-1), proj_dec, proj_dec, proj_dec, *([cache_kt] * n_pages), *([cache_vt] * n_pages))


def _lower_bounds(gamma):
    p = jax.nn.softmax(gamma.astype(F32), axis=0)
    return jnp.cumsum(p, axis=0) - p[:1]


def _tokens_last_to_heads(x_t, depth, batch, seq):
    return jnp.transpose(x_t.reshape(depth, batch, MB_HEADS, MB_DH, seq), (0, 1, 4, 2, 3))


def kernel(x_prompt, x_sample, cache_k, cache_v, state_hgrn, page_table, hg_lower_bounds, w_in, w_out, hg_gnorm,
           norm_ffn1, ffn1_wi, ffn1_wo, norm_mix, norm_ffn2, ffn2_wi, ffn2_wo, final_norm):
    batch, seq, _ = x_prompt.shape
    dec_batch, dec_seq, _ = x_sample.shape
    depth, n_pool = cache_k.shape[:2]
    lower = _lower_bounds(hg_lower_bounds)
    xp = x_prompt.reshape(batch * seq, D_MODEL)
    xs = x_sample.reshape(dec_batch * dec_seq, D_MODEL)
    fn = final_norm.reshape(1, D_MODEL)
    cache_kt = jnp.transpose(cache_k, (0, 1, 3, 4, 2)).reshape(depth * n_pool, MB_WIDTH, PAGE_SIZE)
    cache_vt = jnp.transpose(cache_v, (0, 1, 3, 4, 2)).reshape(depth * n_pool, MB_WIDTH, PAGE_SIZE)
    wi1, wo1, wi2, wo2, w_in16, w_out16 = (a.astype(BF16) for a in (ffn1_wi, ffn1_wo, ffn2_wi, ffn2_wo, w_in, w_out))
    sp, ks, vs = [], [], []
    kv_stacks, state_stack = None, None
    k0, v0 = D_IN - 2 * MB_WIDTH, D_IN - MB_WIDTH
    for l in range(depth):
        lb = lower[l].reshape(1, HG_WIDTH)
        gn = hg_gnorm[l].reshape(1, HG_D)
        n1, nm, n2 = (a[l].reshape(1, D_MODEL) for a in (norm_ffn1, norm_mix, norm_ffn2))
        xp, main, *kv_stacks = _ffn_inproj_prompt(xp, n1, wi1, wo1, nm, w_in16, w_in16[l, :, v0:].T, kv_stacks, l, depth,
                                                  batch, seq)
        o_hg, s_p = _hgrn_prompt(main, lb, gn, batch, seq)
        o_mb = _moba_prompt(main, kv_stacks[1], l, batch, seq)
        xp = _outproj_ffn(xp, o_hg, o_mb, w_out16, n2, wi2, wo2, fn, l, final=(l == depth - 1))
        sp.append(s_p)

        xs, proj = _ffn_inproj(xs, n1, wi1, wo1, nm, w_in16, l)
        proj_dec = proj.reshape(dec_batch, dec_seq, D_IN)
        o_hg, state_stack = _hgrn_decode(proj, lb, gn, state_hgrn, state_stack, l, dec_batch, dec_seq)
        o_mb = _moba_decode(proj_dec, cache_kt, cache_vt, page_table, l, n_pool)
        o_mb = o_mb.reshape(dec_batch * dec_seq, MB_WIDTH).astype(BF16)
        xs = _outproj_ffn(xs, o_hg, o_mb, w_out16, n2, wi2, wo2, fn, l, final=(l == depth - 1))
        ks.append(proj_dec[:, :, k0:v0].reshape(dec_batch, dec_seq, MB_HEADS, MB_DH))
        vs.append(proj_dec[:, :, v0:].reshape(dec_batch, dec_seq, MB_HEADS, MB_DH))
    kp, vp = (_tokens_last_to_heads(a, depth, batch, seq) for a in kv_stacks)
    return (xp.reshape(batch, seq, D_MODEL), xs.reshape(dec_batch, dec_seq, D_MODEL), kp, vp,
            jnp.stack(sp), jnp.stack(ks), jnp.stack(vs), state_stack)
```

```python
import functools

import jax
import jax.numpy as jnp
from jax import lax
from jax.experimental import pallas as pl
from jax.experimental.pallas import tpu as pltpu

F32 = jnp.float32
BF16 = jnp.bfloat16

D_MODEL = 1024
HG_HEADS = 4
HG_D = 128
HG_WIDTH = HG_HEADS * HG_D
MB_HEADS = 8
MB_DH = 64
MB_WIDTH = MB_HEADS * MB_DH
MB_BLOCK = 256
MB_TOPK = 3
D_FF = 2816
D_IN = 4 * HG_WIDTH + 3 * MB_WIDTH
D_MAIN = 4 * HG_WIDTH + 2 * MB_WIDTH
PAGE_SIZE = 128
EPS = 1e-6
FFN_SCALE = 0.5

LANES = 128
MXU_WIDTH = 256
FF_SPLITS = (0, 6 * MXU_WIDTH, D_FF)
DENSE_ROWS = 256
HG_CHUNK = 128
HG_ROWS = 512
HG_DEC_SEQS = 8
MB_GROUP = 2
LOG2_E = 1.4426950408889634
SCORE_SCALE = MB_DH ** -0.5 * LOG2_E
ONES_ROWS = 16
MASKED = -(2.0 ** 30)
VMEM_LIMIT = 56 * 1024 * 1024

NEG_INF = float("-inf")


def _resident(shape):
    return pl.BlockSpec(shape, lambda *_: (0,) * len(shape), pipeline_mode=pl.Buffered(1))


def _layer_resident(shape, layer):
    return pl.BlockSpec((1,) + shape, lambda *_: (layer,) + (0,) * len(shape), pipeline_mode=pl.Buffered(1))


def _params(semantics):
    return pltpu.CompilerParams(dimension_semantics=semantics, vmem_limit_bytes=VMEM_LIMIT)


def _rmsnorm(x, g):
    return x * lax.rsqrt(jnp.mean(x * x, axis=-1, keepdims=True) + EPS) * g


def _silu(x):
    return x * jax.nn.sigmoid(x)


def _dot_nt(a, b, **kw):
    return lax.dot_general(a, b, (((1,), (1,)), ((), ())), preferred_element_type=F32, **kw)


def _dot_tn(a, b, **kw):
    return lax.dot_general(a, b, (((0,), (0,)), ((), ())), preferred_element_type=F32, **kw)


def _swiglu_half_step(x, n_ref, wi_ref, wo_ref):
    h = _rmsnorm(x, n_ref[...]).astype(BF16)
    acc = jnp.zeros_like(x)
    for lo, hi in zip(FF_SPLITS[:-1], FF_SPLITS[1:]):
        a = jnp.dot(h, wi_ref[0, :, lo:hi], preferred_element_type=F32)
        b = jnp.dot(h, wi_ref[0, :, D_FF + lo:D_FF + hi], preferred_element_type=F32)
        g = (_silu(a) * b).astype(BF16)
        acc = acc + jnp.dot(g, wo_ref[0, lo:hi, :], preferred_element_type=F32)
    return x + FFN_SCALE * acc


def _ffn_inproj_kernel(x_ref, n1_ref, wi_ref, wo_ref, nm_ref, win_ref, x1_ref, proj_ref):
    x1 = _swiglu_half_step(x_ref[...], n1_ref, wi_ref, wo_ref)
    x1_ref[...] = x1
    h = _rmsnorm(x1, nm_ref[...]).astype(BF16)
    proj_ref[...] = jnp.dot(h, win_ref[0], preferred_element_type=F32)


def _ffn_inproj(x, n1, wi, wo, nm, w_in, layer):
    n = x.shape[0]
    rows = pl.BlockSpec((DENSE_ROWS, D_MODEL), lambda i: (i, 0))
    return pl.pallas_call(
        _ffn_inproj_kernel,
        grid=(n // DENSE_ROWS,),
        in_specs=[rows, _resident((1, D_MODEL)), _layer_resident((D_MODEL, 2 * D_FF), layer),
                  _layer_resident((D_FF, D_MODEL), layer), _resident((1, D_MODEL)), _layer_resident((D_MODEL, D_IN), layer)],
        out_specs=[rows, pl.BlockSpec((DENSE_ROWS, D_IN), lambda i: (i, 0))],
        out_shape=[jax.ShapeDtypeStruct((n, D_MODEL), F32), jax.ShapeDtypeStruct((n, D_IN), F32)],
        compiler_params=_params(("parallel",)),
        name="ffn_inproj",
    )(x, n1, wi, wo, nm, w_in)


def _ffn_inproj_prompt_kernel(x_ref, n1_ref, wi_ref, wo_ref, nm_ref, win_ref, wvt_ref, *rest):
    x1_ref, main_ref, kt_ref, vt_ref = rest[-4:]
    x1 = _swiglu_half_step(x_ref[...], n1_ref, wi_ref, wo_ref)
    x1_ref[...] = x1
    h = _rmsnorm(x1, nm_ref[...]).astype(BF16)
    main = jnp.dot(h, win_ref[0, :, :D_MAIN], preferred_element_type=F32)
    main_ref[...] = main
    kt_ref[0, 0] = main[:, D_MAIN - MB_WIDTH:].T
    vt_ref[0, 0] = _dot_nt(wvt_ref[...], h)


def _ffn_inproj_prompt(x, n1, wi, wo, nm, w_in, w_vt, stacks, layer, depth, batch, seq):
    steps = seq // DENSE_ROWS
    rows = pl.BlockSpec((DENSE_ROWS, D_MODEL), lambda b, i: (b * steps + i, 0))
    tr = pl.BlockSpec((1, 1, MB_WIDTH, DENSE_ROWS), lambda b, i: (layer, b, 0, i))
    stack = jax.ShapeDtypeStruct((depth, batch, MB_WIDTH, seq), F32)
    prev = () if stacks is None else tuple(stacks)
    return pl.pallas_call(
        _ffn_inproj_prompt_kernel,
        grid=(batch, steps),
        in_specs=[rows, _resident((1, D_MODEL)), _layer_resident((D_MODEL, 2 * D_FF), layer),
                  _layer_resident((D_FF, D_MODEL), layer), _resident((1, D_MODEL)), _layer_resident((D_MODEL, D_IN), layer),
                  _resident((MB_WIDTH, D_MODEL))]
        + [pl.BlockSpec(memory_space=pl.ANY)] * len(prev),
        out_specs=[rows, pl.BlockSpec((DENSE_ROWS, D_MAIN), lambda b, i: (b * steps + i, 0)), tr, tr],
        out_shape=[jax.ShapeDtypeStruct((batch * seq, D_MODEL), F32), jax.ShapeDtypeStruct((batch * seq, D_MAIN), F32),
                   stack, stack],
        input_output_aliases={7: 2, 8: 3} if prev else {},
        compiler_params=_params(("parallel", "parallel")),
        name="ffn_inproj_prompt",
    )(x, n1, wi, wo, nm, w_in, w_vt, *prev)


def _outproj_ffn_kernel(x_ref, ohg_ref, omb_ref, wout_ref, n2_ref, wi_ref, wo_ref, fn_ref, y_ref, *, final):
    x = x_ref[...]
    x = x + jnp.dot(ohg_ref[...], wout_ref[0, :HG_WIDTH, :], preferred_element_type=F32)
    x = x + jnp.dot(omb_ref[...], wout_ref[0, HG_WIDTH:, :], preferred_element_type=F32)
    x = _swiglu_half_step(x, n2_ref, wi_ref, wo_ref)
    if final:
        x = _rmsnorm(x, fn_ref[...])
    y_ref[...] = x


def _outproj_ffn(x, o_hg, o_mb, w_out, n2, wi, wo, fn, layer, final):
    n = x.shape[0]
    rows = pl.BlockSpec((DENSE_ROWS, D_MODEL), lambda i: (i, 0))
    half = pl.BlockSpec((DENSE_ROWS, HG_WIDTH), lambda i: (i, 0))
    return pl.pallas_call(
        functools.partial(_outproj_ffn_kernel, final=final),
        grid=(n // DENSE_ROWS,),
        in_specs=[rows, half, half, _layer_resident((D_MODEL, D_MODEL), layer), _resident((1, D_MODEL)),
                  _layer_resident((D_MODEL, 2 * D_FF), layer), _layer_resident((D_FF, D_MODEL), layer),
                  _resident((1, D_MODEL))],
        out_specs=rows,
        out_shape=jax.ShapeDtypeStruct((n, D_MODEL), F32),
        compiler_params=_params(("parallel",)),
        name="outproj_ffn",
    )(x, o_hg, o_mb, w_out, n2, wi, wo, fn)


def _segment_products(f, top):
    rows = f.shape[0]
    t = lax.broadcasted_iota(jnp.int32, f.shape, 0)
    gq, gk, tot = f, jnp.ones_like(f), f
    out = {1: (gq, gk)}
    m = 1
    while m < top:
        second = (t & m) != 0
        up = pltpu.roll(tot, m, 0)
        dn = pltpu.roll(tot, rows - m, 0)
        gq = jnp.where(second, gq * up, gq)
        gk = jnp.where(second, gk, gk * dn)
        tot = gq * gk
        m *= 2
        out[m] = (gq, gk)
    return out


def _intra_scores(q, k, prods, top):
    rows = q.shape[0]
    t = lax.broadcasted_iota(jnp.int32, (rows, rows), 0)
    s = lax.broadcasted_iota(jnp.int32, (rows, rows), 1)
    ts = t ^ s
    a = jnp.where(t == s, _dot_nt(q.astype(BF16), k.astype(BF16)), 0.0)
    m = 1
    while m < top:
        gq, gk = prods[m]
        p = _dot_nt((q * gq).astype(BF16), (k * gk).astype(BF16))
        a = jnp.where((ts >= m) & (ts < 2 * m) & (t > s), p, a)
        m *= 2
    return a


def _hgrn_inputs(hq, z, lb):
    sig = jax.nn.sigmoid(z)
    f = lb + (1.0 - lb) * sig
    k = (1.0 - lb) * jax.nn.sigmoid(-z)
    return _silu(hq), k, f


def _hgrn_finish(o, hg, gn):
    return _rmsnorm(o, gn) * _silu(hg)


def _hgrn_prompt_kernel(hq_ref, hf_ref, hi_ref, hg_ref, lb_ref, gn_ref, o_ref, s_ref, st_ref):
    step = pl.program_id(1)

    @pl.when(step == 0)
    def _():
        st_ref[...] = jnp.zeros_like(st_ref)

    def chunk_body(j, carry):
        r0 = pl.multiple_of(j * HG_CHUNK, HG_CHUNK)
        rs = pl.ds(r0, HG_CHUNK)
        for h in range(HG_HEADS):
            cs = slice(h * HG_D, (h + 1) * HG_D)
            q, k, f = _hgrn_inputs(hq_ref[rs, cs], hf_ref[rs, cs], lb_ref[:, cs])
            v = hi_ref[rs, cs].astype(BF16)
            prods = _segment_products(f, HG_CHUNK)
            a = _intra_scores(q, k, prods, HG_CHUNK)
            gq, gk = prods[HG_CHUNK]
            st = st_ref[h]
            o = jnp.dot(a.astype(BF16), v, preferred_element_type=F32)
            o = o + _dot_nt((q * gq).astype(BF16), st.astype(BF16))
            st_ref[h] = st * gq[HG_CHUNK - 1:HG_CHUNK, :] + _dot_tn(v, (k * gk).astype(BF16))
            o_ref[rs, cs] = _hgrn_finish(o, hg_ref[rs, cs], gn_ref[...]).astype(o_ref.dtype)
        return carry

    lax.fori_loop(0, HG_ROWS // HG_CHUNK, chunk_body, 0)

    @pl.when(step == pl.num_programs(1) - 1)
    def _():
        for h in range(HG_HEADS):
            s_ref[0, h] = st_ref[h].T


def _hgrn_prompt(proj, lb, gn, batch, seq):
    steps = seq // HG_ROWS

    def col(c):
        return pl.BlockSpec((HG_ROWS, HG_WIDTH), lambda b, i: (b * steps + i, c))

    return pl.pallas_call(
        _hgrn_prompt_kernel,
        grid=(batch, steps),
        in_specs=[col(0), col(1), col(2), col(3), _resident((1, HG_WIDTH)), _resident((1, HG_D))],
        out_specs=[pl.BlockSpec((HG_ROWS, HG_WIDTH), lambda b, i: (b * steps + i, 0)),
                   pl.BlockSpec((1, HG_HEADS, HG_D, HG_D), lambda b, i: (b, 0, 0, 0))],
        out_shape=[jax.ShapeDtypeStruct((batch * seq, HG_WIDTH), BF16),
                   jax.ShapeDtypeStruct((batch, HG_HEADS, HG_D, HG_D), F32)],
        scratch_shapes=[pltpu.VMEM((HG_HEADS, HG_D, HG_D), F32)],
        compiler_params=_params(("parallel", "arbitrary")),
        name="hgrn_prompt",
    )(proj, proj, proj, proj, lb, gn)


def _hgrn_decode_kernel(hq_ref, hf_ref, hi_ref, hg_ref, lb_ref, gn_ref, s0_ref, *rest, seq):
    o_ref, s_ref = rest[-2:]
    rows = HG_DEC_SEQS * seq
    row = lax.broadcasted_iota(jnp.int32, (rows, HG_D), 0)
    for h in range(HG_HEADS):
        cs = slice(h * HG_D, (h + 1) * HG_D)
        q, k, f = _hgrn_inputs(hq_ref[:, cs], hf_ref[:, cs], lb_ref[:, cs])
        v = hi_ref[:, cs].astype(BF16)
        prods = _segment_products(f, seq)
        a = _intra_scores(q, k, prods, seq)
        gq, gk = prods[seq]
        qg, kg = q * gq, k * gk
        o = jnp.dot(a.astype(BF16), v, preferred_element_type=F32)
        for i in range(HG_DEC_SEQS):
            mine = (row >= i * seq) & (row < (i + 1) * seq)
            s0 = s0_ref[0, i, h]
            o = o + jnp.dot(jnp.where(mine, qg, 0.0).astype(BF16), s0.astype(BF16), preferred_element_type=F32)
            last = (i + 1) * seq - 1
            decay = jnp.broadcast_to(gq[last:last + 1, :], (HG_D, HG_D)).T
            s_ref[0, i, h] = s0 * decay + _dot_tn(jnp.where(mine, kg, 0.0).astype(BF16), v)
        o_ref[:, cs] = _hgrn_finish(o, hg_ref[:, cs], gn_ref[...]).astype(o_ref.dtype)


def _hgrn_decode(proj, lb, gn, state, stack, layer, batch, seq):
    rows = HG_DEC_SEQS * seq
    prev = () if stack is None else (stack,)

    def col(c):
        return pl.BlockSpec((rows, HG_WIDTH), lambda g: (g, c))

    return pl.pallas_call(
        functools.partial(_hgrn_decode_kernel, seq=seq),
        grid=(batch // HG_DEC_SEQS,),
        in_specs=[col(0), col(1), col(2), col(3), _resident((1, HG_WIDTH)), _resident((1, HG_D)),
                  pl.BlockSpec((1, HG_DEC_SEQS, HG_HEADS, HG_D, HG_D), lambda g: (layer, g, 0, 0, 0))]
        + [pl.BlockSpec(memory_space=pl.ANY)] * len(prev),
        out_specs=[pl.BlockSpec((rows, HG_WIDTH), lambda g: (g, 0)),
                   pl.BlockSpec((1, HG_DEC_SEQS, HG_HEADS, HG_D, HG_D), lambda g: (layer, g, 0, 0, 0))],
        out_shape=[jax.ShapeDtypeStruct((batch * seq, HG_WIDTH), BF16), jax.ShapeDtypeStruct(state.shape, F32)],
        input_output_aliases={7: 1} if prev else {},
        compiler_params=_params(("parallel",)),
        name="hgrn_decode",
    )(proj, proj, proj, proj, lb, gn, state, *prev)


def _top_columns(gates):
    n = len(gates)
    g = list(gates)
    picked = [None] * n
    for _ in range(MB_TOPK):
        mx = functools.reduce(jnp.maximum, g)
        idx = functools.reduce(jnp.minimum, [jnp.where(g[j] == mx, float(j), float(n)) for j in range(n)])
        for j in range(n):
            hit = (idx == float(j)) & (mx > NEG_INF)
            picked[j] = hit if picked[j] is None else picked[j] | hit
            g[j] = jnp.where(hit, NEG_INF, g[j])
    return picked


def _picked_offsets_by_row(gate_t, n_valid):
    rowi = lax.broadcasted_iota(jnp.int32, gate_t.shape, 0)
    rowf = rowi.astype(F32)
    g = jnp.where(rowi < n_valid, gate_t, NEG_INF)
    offsets = jnp.full(gate_t.shape, MASKED, F32)
    for _ in range(MB_TOPK):
        mx = jnp.max(g, axis=0, keepdims=True)
        idx = jnp.min(jnp.where(g == mx, rowf, float(LANES)), axis=0, keepdims=True)
        hit = (rowf == idx) & (mx > NEG_INF)
        offsets = jnp.where(hit, 0.0, offsets)
        g = jnp.where(hit, NEG_INF, g)
    return offsets


def _moba_prompt_kernel(q_ref, qn_ref, k_ref, vt_ref, o_ref, ka_ref, kb_ref, va_ref, vb_ref, km_ref, s_ref, qsel_ref, *,
                        n_blocks):
    i = pl.program_id(2)
    lane = lax.broadcasted_iota(jnp.int32, (MB_BLOCK, LANES), 1)
    first = lane < MB_DH
    feat = lax.broadcasted_iota(jnp.int32, (LANES, MB_BLOCK), 0)
    top = feat < MB_DH

    @pl.when(i == 0)
    def _():
        qsel_ref[0] = jnp.zeros(qsel_ref.shape[1:], BF16)

        def prep(j, carry):
            rs = pl.ds(pl.multiple_of(j * MB_BLOCK, MB_BLOCK), MB_BLOCK)
            k = k_ref[rs, :]
            vt = vt_ref[0, 0, :, rs]
            ka_ref[rs, :] = jnp.where(first, k, jnp.where(lane == MB_DH + j, 1.0, 0.0)).astype(BF16)
            kb_ref[rs, :] = jnp.where(first, jnp.where(lane == j, 1.0, 0.0), k).astype(BF16)
            va_ref[:, rs] = jnp.where(top, vt, 1.0).astype(BF16)
            vb_ref[:, rs] = jnp.where(top, 1.0, vt).astype(BF16)
            km_ref[pl.ds(j, 1), :] = jnp.mean(k, axis=0, keepdims=True)
            return carry

        lax.fori_loop(0, n_blocks, prep, 0)

    q = q_ref[...] * SCORE_SCALE
    heads = [(jnp.where(first, q, 0.0).astype(BF16), qsel_ref[i % 2, 0], ka_ref, va_ref),
             (jnp.where(first, 0.0, q).astype(BF16), qsel_ref[i % 2, 1], kb_ref, vb_ref)]
    v_rows = (slice(0, MB_DH + ONES_ROWS), slice(MB_DH - ONES_ROWS, LANES))

    def select_for_next_tile():
        q_next = qn_ref[...] * (MB_DH ** -0.5)
        for h, is_a in enumerate((True, False)):
            qh = jnp.where(first, q_next, 0.0) if is_a else jnp.where(first, 0.0, q_next)
            gate_t = _dot_nt(km_ref[...], qh, precision=lax.Precision.HIGHEST)
            offs = _picked_offsets_by_row(gate_t, i + 1)
            before = MB_DH if is_a else 0
            slots = ([jnp.zeros((before, MB_BLOCK), F32)] if before else []) + [offs]
            slots.append(jnp.zeros((LANES - before - n_blocks, MB_BLOCK), F32))
            sel = jnp.concatenate(slots, axis=0).T
            q_feat = q_next * LOG2_E
            q_sel = jnp.where(first, q_feat, sel) if is_a else jnp.where(first, sel, q_feat)
            qsel_ref[(i + 1) % 2, h] = q_sel.astype(BF16)

    group_rows = MB_GROUP * MB_BLOCK
    last_group = n_blocks // MB_GROUP - 1

    def group(g):
        return pl.ds(pl.multiple_of(g * group_rows, group_rows), group_rows)

    def produce(g, slot):
        rs = group(jnp.minimum(g, last_group))
        cmax = []
        for h, (_, q_sel, k_ref_h, _) in enumerate(heads):
            s = _dot_nt(k_ref_h[rs, :], q_sel)
            s_ref[slot, h] = s
            cmax.append(jnp.max(s, axis=0, keepdims=True))
        return cmax

    def consume(g, slot, cmax, state):
        rs = group(g)
        out = []
        for h, (_, _, _, v_ref_h) in enumerate(heads):
            m, acc = state[2 * h:2 * h + 2]
            m_new = jnp.maximum(m, cmax[h])
            p = jnp.exp2(s_ref[slot, h] - m_new).astype(BF16)
            out += [m_new, jnp.exp2(m - m_new) * acc + jnp.dot(v_ref_h[v_rows[h], rs], p, preferred_element_type=F32)]
        return out

    def two_groups(t, carry):
        cmax0, state = list(carry[:2]), list(carry[2:])
        cmax1 = produce(2 * t + 1, 1)
        state = consume(2 * t, 0, cmax0, state)
        cmax0 = produce(2 * t + 2, 0)
        state = consume(2 * t + 1, 1, cmax1, state)
        return tuple(cmax0 + state)

    select_for_next_tile()
    own = pl.ds(pl.multiple_of(i * MB_BLOCK, MB_BLOCK), MB_BLOCK)
    key = lax.broadcasted_iota(jnp.int32, (MB_BLOCK, MB_BLOCK), 0)
    qry = lax.broadcasted_iota(jnp.int32, (MB_BLOCK, MB_BLOCK), 1)
    own_scores = [jnp.where(key <= qry, _dot_nt(k_ref_h[own, :], q_own), NEG_INF) for q_own, _, k_ref_h, _ in heads]
    cmax0 = produce(0, 0)
    init = []
    for h, (s, (_, _, _, v_ref_h)) in enumerate(zip(own_scores, heads)):
        m = jnp.max(s, axis=0, keepdims=True)
        init += [m, jnp.dot(v_ref_h[v_rows[h], own], jnp.exp2(s - m).astype(BF16), preferred_element_type=F32)]
    n_groups = (i + MB_GROUP - 1) // MB_GROUP
    res = lax.fori_loop(0, n_groups // 2, two_groups, tuple(cmax0 + init))

    def last_group(_, carry):
        return tuple(list(carry[:2]) + consume(n_groups - 1, 0, list(carry[:2]), list(carry[2:])))

    res = lax.fori_loop(0, n_groups % 2, last_group, res)
    acc_a, acc_b = res[3], res[5]
    num = jnp.concatenate([acc_a[:MB_DH], acc_b[ONES_ROWS:]], axis=0)
    den = jnp.where(top, acc_a[MB_DH:MB_DH + 1, :], acc_b[:1, :])
    o_ref[...] = (num / den).T.astype(o_ref.dtype)


def _moba_prompt(main, vt_stack, layer, batch, seq):
    n_blocks = seq // MB_BLOCK
    assert n_blocks % 8 == 0 and n_blocks % (2 * MB_GROUP) == 0 and n_blocks <= MB_DH
    pairs = MB_WIDTH // LANES
    q0 = 4 * HG_WIDTH // LANES
    return pl.pallas_call(
        functools.partial(_moba_prompt_kernel, n_blocks=n_blocks),
        grid=(batch, pairs, n_blocks),
        in_specs=[pl.BlockSpec((MB_BLOCK, LANES), lambda b, p, i: (b * n_blocks + i, q0 + p)),
                  pl.BlockSpec((MB_BLOCK, LANES),
                               lambda b, p, i: (b * n_blocks + jnp.minimum(i + 1, n_blocks - 1), q0 + p)),
                  pl.BlockSpec((seq, LANES), lambda b, p, i: (b, q0 + pairs + p)),
                  pl.BlockSpec((1, 1, LANES, seq), lambda b, p, i: (layer, b, p, 0))],
        out_specs=pl.BlockSpec((MB_BLOCK, LANES), lambda b, p, i: (b * n_blocks + i, p)),
        out_shape=jax.ShapeDtypeStruct((batch * seq, MB_WIDTH), BF16),
        scratch_shapes=[pltpu.VMEM((seq, LANES), BF16)] * 2 + [pltpu.VMEM((LANES, seq), BF16)] * 2
        + [pltpu.VMEM((n_blocks, LANES), F32), pltpu.VMEM((2, 2, MB_GROUP * MB_BLOCK, MB_BLOCK), F32),
           pltpu.VMEM((2, 2, MB_BLOCK, LANES), BF16)],
        compiler_params=_params(("parallel", "parallel", "arbitrary")),
        name="moba_prompt",
    )(main, main, main, vt_stack)


def _moba_decode_kernel(pt_ref, q_ref, k_ref, v_ref, *refs, n_pages, seq):
    del pt_ref
    k_pages, v_pages, o_ref = refs[:n_pages], refs[n_pages:2 * n_pages], refs[2 * n_pages]
    pages_per_block = MB_BLOCK // PAGE_SIZE
    n_blocks = n_pages // pages_per_block
    rows = seq * MB_HEADS
    sub = lax.broadcasted_iota(jnp.int32, (MB_HEADS, MB_WIDTH), 0)
    head_of_lane = lax.broadcasted_iota(jnp.int32, (MB_HEADS, MB_WIDTH), 1) // MB_DH
    head_mask = sub == head_of_lane
    big = lax.broadcasted_iota(jnp.int32, (LANES, MB_WIDTH), 0)

    def rows_to_tile(x):
        out = jnp.zeros((LANES, MB_WIDTH), F32)
        for t in range(x.shape[0]):
            out = jnp.where(big == t, jnp.broadcast_to(x[t:t + 1, :], out.shape), out)
        return out

    q = q_ref[0] * (MB_DH ** -0.5)
    qbd = jnp.concatenate(
        [jnp.where(head_mask, jnp.broadcast_to(q[t:t + 1, :], head_mask.shape), 0.0) for t in range(seq)], axis=0)
    qbd16 = qbd.astype(BF16)

    raw = [jnp.dot(qbd16, k_pages[p][0].astype(BF16), preferred_element_type=F32) for p in range(n_pages)]
    glane = lax.broadcasted_iota(jnp.int32, (rows, LANES), 1)
    r = lax.broadcasted_iota(jnp.int32, (rows, LANES), 0) // MB_HEADS
    own_scores = jnp.where(glane <= r, _dot_nt(qbd16, rows_to_tile(k_ref[0]).astype(BF16)), NEG_INF)

    gates = []
    for j in range(n_blocks):
        tot = sum(raw[j * pages_per_block + u] for u in range(pages_per_block))
        gates.append(jnp.sum(tot, axis=1, keepdims=True) * (1.0 / MB_BLOCK))
    bias = [jnp.where(hit, 0.0, NEG_INF) for hit in _top_columns(gates)]
    scores = [raw[p] + bias[p // pages_per_block] for p in range(n_pages)] + [own_scores]

    m = functools.reduce(jnp.maximum, scores)
    m = jnp.max(m, axis=-1, keepdims=True)
    l = jnp.zeros((rows, 1), F32)
    acc = jnp.zeros((rows, MB_WIDTH), F32)
    for p, s in enumerate(scores):
        w = jnp.exp(s - m)
        l = l + jnp.sum(w, axis=-1, keepdims=True)
        if p < n_pages:
            acc = acc + _dot_nt(w.astype(BF16), v_pages[p][0].astype(BF16))
        else:
            acc = acc + jnp.dot(w.astype(BF16), rows_to_tile(v_ref[0]).astype(BF16), preferred_element_type=F32)
    acc = acc / l
    for t in range(seq):
        own = jnp.where(head_mask, acc[t * MB_HEADS:(t + 1) * MB_HEADS, :], 0.0)
        o_ref[0, t:t + 1, :] = jnp.sum(own, axis=0, keepdims=True)


def _moba_decode(proj_dec, cache_kt, cache_vt, page_table, layer, n_pool):
    batch, seq, _ = proj_dec.shape
    n_pages = page_table.shape[1]
    q0 = 4 * HG_WIDTH // MB_WIDTH

    def new(c):
        return pl.BlockSpec((1, seq, MB_WIDTH), lambda b, pt: (b, 0, q0 + c))

    def page(j):
        return pl.BlockSpec((1, MB_WIDTH, PAGE_SIZE), lambda b, pt: (layer * n_pool + pt[b * n_pages + j], 0, 0))

    pages = [page(j) for j in range(n_pages)]
    grid_spec = pltpu.PrefetchScalarGridSpec(
        num_scalar_prefetch=1,
        grid=(batch,),
        in_specs=[new(0), new(1), new(2)] + pages + pages,
        out_specs=pl.BlockSpec((1, seq, MB_WIDTH), lambda b, pt: (b, 0, 0)),
    )
    return pl.pallas_call(
        functools.partial(_moba_decode_kernel, n_pages=n_pages, seq=seq),
        grid_spec=grid_spec,
        out_shape=jax.ShapeDtypeStruct((batch, seq, MB_WIDTH), F32),
        compiler_params=_params(("parallel",)),
        name="moba_decode",
    )(page_table.reshape(-1), proj_dec, proj_dec, proj_dec, *([cache_kt] * n_pages), *([cache_vt] * n_pages))


def _lower_bounds(gamma):
    p = jax.nn.softmax(gamma.astype(F32), axis=0)
    return jnp.cumsum(p, axis=0) - p[:1]


def _tokens_last_to_heads(x_t, depth, batch, seq):
    return jnp.transpose(x_t.reshape(depth, batch, MB_HEADS, MB_DH, seq), (0, 1, 4, 2, 3))


def kernel(x_prompt, x_sample, cache_k, cache_v, state_hgrn, page_table, hg_lower_bounds, w_in, w_out, hg_gnorm,
           norm_ffn1, ffn1_wi, ffn1_wo, norm_mix, norm_ffn2, ffn2_wi, ffn2_wo, final_norm):
    batch, seq, _ = x_prompt.shape
    dec_batch, dec_seq, _ = x_sample.shape
    depth, n_pool = cache_k.shape[:2]
    lower = _lower_bounds(hg_lower_bounds)
    xp = x_prompt.reshape(batch * seq, D_MODEL)
    xs = x_sample.reshape(dec_batch * dec_seq, D_MODEL)
    fn = final_norm.reshape(1, D_MODEL)
    cache_kt = jnp.transpose(cache_k, (0, 1, 3, 4, 2)).reshape(depth * n_pool, MB_WIDTH, PAGE_SIZE)
    cache_vt = jnp.transpose(cache_v, (0, 1, 3, 4, 2)).reshape(depth * n_pool, MB_WIDTH, PAGE_SIZE)
    wi1, wo1, wi2, wo2, w_in16, w_out16 = (a.astype(BF16) for a in (ffn1_wi, ffn1_wo, ffn2_wi, ffn2_wo, w_in, w_out))
    sp, ks, vs = [], [], []
    kv_stacks, state_stack = None, None
    k0, v0 = D_IN - 2 * MB_WIDTH, D_IN - MB_WIDTH
    for l in range(depth):
        lb = lower[l].reshape(1, HG_WIDTH)
        gn = hg_gnorm[l].reshape(1, HG_D)
        n1, nm, n2 = (a[l].reshape(1, D_MODEL) for a in (norm_ffn1, norm_mix, norm_ffn2))
        xp, main, *kv_stacks = _ffn_inproj_prompt(xp, n1, wi1, wo1, nm, w_in16, w_in16[l, :, v0:].T, kv_stacks, l, depth,
                                                  batch, seq)
        o_hg, s_p = _hgrn_prompt(main, lb, gn, batch, seq)
        o_mb = _moba_prompt(main, kv_stacks[1], l, batch, seq)
        xp = _outproj_ffn(xp, o_hg, o_mb, w_out16, n2, wi2, wo2, fn, l, final=(l == depth - 1))
        sp.append(s_p)

        xs, proj = _ffn_inproj(xs, n1, wi1, wo1, nm, w_in16, l)
        proj_dec = proj.reshape(dec_batch, dec_seq, D_IN)
        o_hg, state_stack = _hgrn_decode(proj, lb, gn, state_hgrn, state_stack, l, dec_batch, dec_seq)
        o_mb = _moba_decode(proj_dec, cache_kt, cache_vt, page_table, l, n_pool)
        o_mb = o_mb.reshape(dec_batch * dec_seq, MB_WIDTH).astype(BF16)
        xs = _outproj_ffn(xs, o_hg, o_mb, w_out16, n2, wi2, wo2, fn, l, final=(l == depth - 1))
        ks.append(proj_dec[:, :, k0:v0].reshape(dec_batch, dec_seq, MB_HEADS, MB_DH))
        vs.append(proj_dec[:, :, v0:].reshape(dec_batch, dec_seq, MB_HEADS, MB_DH))
    kp, vp = (_tokens_last_to_heads(a, depth, batch, seq) for a in kv_stacks)
    return (xp.reshape(batch, seq, D_MODEL), xs.reshape(dec_batch, dec_seq, D_MODEL), kp, vp,
            jnp.stack(sp), jnp.stack(ks), jnp.stack(vs), state_stack)
```
